```python
import jax, jax.numpy as jnp
from jax import lax
import numpy as np

D_MODEL = 1024
BATCH = 16
SEQ = 256
DEPTH = 2
DEC_BATCH = 8
DEC_SEQ = 2048
PAST_LEN = 512

GRID_W = 64
W_A = 256
W_B = 256
W_C = 512
HEAD_C = 64
H_C = W_C // HEAD_C
D_MIX = W_A + W_B + W_C
CONV_A = 3
CONV_B = 31
LORA_W = 64
LORA_A = 64
N_DIR = 2
OFF_B = 4 * W_A
OFF_C = OFF_B + 3 * W_B
SHIFT_W = 3 * W_C + N_DIR * LORA_W + N_DIR * LORA_A
OFF_G = OFF_C + SHIFT_W
D_IN = OFF_G + W_C
RMS_EPS = 1e-6
LN_EPS = 1e-5
GN_EPS = 64e-5

kernel_name = "hybrid_conv_rwkv7_flow_step"


def rms_norm(x, g):
    xf = x.astype(jnp.float32)
    y = xf * lax.rsqrt(jnp.mean(xf * xf, -1, keepdims=True) + RMS_EPS)
    return (y * g.astype(jnp.float32)).astype(x.dtype)


def layer_norm(x, g, b, eps):
    xf = x.astype(jnp.float32)
    mu = jnp.mean(xf, -1, keepdims=True)
    var = jnp.mean(jnp.square(xf - mu), -1, keepdims=True)
    y = (xf - mu) * lax.rsqrt(var + eps)
    return (y * g.astype(jnp.float32) + b.astype(jnp.float32)).astype(x.dtype)


def dwconv(x, w):
    k, ch = w.shape
    return lax.conv_general_dilated(x, w[:, None, :].astype(x.dtype), window_strides=(1,),
                                    padding=[(k // 2, k // 2)],
                                    dimension_numbers=('NWC', 'WIO', 'NWC'),
                                    feature_group_count=ch)


def conv_seq(x, w, latent, along_rows):
    if not latent:
        return dwconv(x, w)
    bn, t, ch = x.shape
    rows = t // GRID_W
    g = x.reshape(bn, rows, GRID_W, ch)
    if along_rows:
        return dwconv(g.reshape(bn * rows, GRID_W, ch), w).reshape(bn, t, ch)
    gt = g.transpose(0, 2, 1, 3).reshape(bn * GRID_W, rows, ch)
    y = dwconv(gt, w).reshape(bn, GRID_W, rows, ch).transpose(0, 2, 1, 3)
    return y.reshape(bn, t, ch)


def centred_shift(p, mu):
    prev = jnp.pad(p[:, :-1], ((0, 0), (1, 0), (0, 0)))
    nxt = jnp.pad(p[:, 1:], ((0, 0), (0, 1), (0, 0)))
    return p + mu * (0.5 * (prev + nxt) - p)


def rwkv7_scan(s0, r, decay, k, v, kk, a, reverse):
    def step(s, inp):
        r_t, w_t, k_t, v_t, kk_t, a_t = inp
        sa = jnp.einsum('bhvk,bhk->bhv', s, kk_t)
        s = (s * w_t[:, :, None, :] - sa[..., None] * (kk_t * a_t)[:, :, None, :]
             + v_t[..., None] * k_t[:, :, None, :])
        return s, jnp.einsum('bhvk,bhk->bhv', s, r_t)
    xs = tuple(jnp.swapaxes(t, 0, 1) for t in (r, decay, k, v, kk, a))
    s, ys = lax.scan(step, s0, xs, reverse=reverse)
    return s, jnp.swapaxes(ys, 0, 1)


def trunk_layer(x, mod, p, latent, s0_fwd, s0_bwd):
    bn, t, _ = x.shape
    dt = x.dtype
    ada = jax.nn.silu(mod) @ p['ada_w'] + p['ada_b']
    ada = ada[None, None] if mod.ndim == 1 else ada[:, None]
    shift, scale, gate = jnp.split(ada, 3, -1)
    h = rms_norm(x, p['g_pre']) * (1.0 + scale) + shift
    z = h @ p['w_in']
    za, zb, zc, gc = jnp.split(z, [OFF_B, OFF_C, OFF_G], -1)
    xa, bg, cg, ga = jnp.split(za, 4, -1)
    ya = bg * conv_seq(cg * xa, p['conv_a'], latent, True) * jax.nn.silu(ga)
    u1, u2, gb = jnp.split(zb, 3, -1)
    u = conv_seq(u1 * jax.nn.sigmoid(u2), p['conv_b'], latent, False) + p['conv_b_bias']
    u = layer_norm(u, p['ln_b_g'], p['ln_b_b'], LN_EPS)
    yb = jax.nn.silu(u) * jax.nn.silu(gb)
    zc = centred_shift(zc, p['mu'])
    r, k, v, lw, la = jnp.split(zc, [W_C, 2 * W_C, 3 * W_C, 3 * W_C + N_DIR * LORA_W], -1)
    lw = lw.reshape(bn, t, N_DIR, LORA_W).astype(jnp.float32)
    la = la.reshape(bn, t, N_DIR, LORA_A).astype(jnp.float32)
    rf, kf, vf = (q.astype(jnp.float32) for q in (r, k, v))
    wl = p['w0'].astype(jnp.float32) + jnp.einsum('btdr,drc->btdc', jnp.tanh(lw), p['w2'].astype(jnp.float32))
    decay = jnp.exp(-jnp.exp(-jax.nn.softplus(-wl) - 0.5))
    a = jax.nn.sigmoid(p['a0'].astype(jnp.float32) + jnp.einsum('btdr,drc->btdc', la, p['a2'].astype(jnp.float32)))
    kk = (kf * p['k_k'].astype(jnp.float32)).reshape(bn, t, H_C, HEAD_C)
    kk = kk * lax.rsqrt(jnp.sum(kk * kk, -1, keepdims=True) + 1e-12)
    kd = kf[:, :, None, :] * (1.0 + (a - 1.0) * p['k_a'].astype(jnp.float32))
    hd = lambda q: q.reshape(bn, t, H_C, HEAD_C)
    rh, vh = hd(rf), hd(vf)
    r_k = p['r_k'].astype(jnp.float32)
    s_fwd, y_fwd = rwkv7_scan(s0_fwd.astype(jnp.float32), rh, hd(decay[:, :, 0]), hd(kd[:, :, 0]),
                              vh, kk, hd(a[:, :, 0]), False)
    s_bwd, y_bwd = rwkv7_scan(s0_bwd.astype(jnp.float32), rh, hd(decay[:, :, 1]), hd(kd[:, :, 1]),
                              vh, kk, hd(a[:, :, 1]), True)
    bonus = (jnp.sum(rh * (hd(kd[:, :, 0]) + hd(kd[:, :, 1])) * r_k, -1, keepdims=True)) * vh
    ys = y_fwd + y_bwd
    mu_y = jnp.mean(ys, -1, keepdims=True)
    var_y = jnp.mean(jnp.square(ys - mu_y), -1, keepdims=True)
    yg = ((ys - mu_y) * lax.rsqrt(var_y + GN_EPS)).reshape(bn, t, W_C)
    yg = yg * p['gn_g'].astype(jnp.float32) + p['gn_b'].astype(jnp.float32)
    yc = (yg + bonus.reshape(bn, t, W_C)).astype(dt) * jax.nn.silu(gc)
    out = jnp.concatenate([ya, yb, yc], -1) @ p['w_out']
    x = x + gate * rms_norm(out, p['g_post'])
    return x, s_fwd.astype(dt), s_bwd.astype(dt)


def setup_inputs(seed: int = 0) -> dict:
    key = jax.random.key(seed)
    ks = jax.random.split(key, 32)
    nrm = lambda i, shape, s: jax.random.normal(ks[i], shape, jnp.float32) * s
    L = DEPTH
    return {
        'x_prompt': nrm(0, (BATCH, SEQ, D_MODEL), 1.0),
        'x_sample': nrm(1, (DEC_BATCH, DEC_SEQ, D_MODEL), 1.0),
        'c': nrm(2, (DEC_BATCH, D_MODEL), 1.0),
        'state_rwkv': nrm(3, (DEC_BATCH, DEPTH, N_DIR, H_C, HEAD_C, HEAD_C), 0.3),
        'c_ctx': nrm(4, (D_MODEL,), 1.0),
        'ada_w': nrm(5, (L, D_MODEL, 3 * D_MODEL), 0.5 * D_MODEL ** -0.5),
        'ada_b': nrm(6, (L, 3 * D_MODEL), 0.02),
        'g_pre': 1.0 + nrm(7, (L, D_MODEL), 0.05),
        'g_post': 1.0 + nrm(8, (L, D_MODEL), 0.05),
        'w_in': nrm(9, (L, D_MODEL, D_IN), D_MODEL ** -0.5),
        'conv_a': nrm(10, (L, CONV_A, W_A), CONV_A ** -0.5),
        'conv_b': nrm(11, (L, CONV_B, W_B), CONV_B ** -0.5),
        'conv_b_bias': nrm(12, (L, W_B), 0.02),
        'ln_b_g': 1.0 + nrm(13, (L, W_B), 0.05),
        'ln_b_b': nrm(14, (L, W_B), 0.02),
        'mu': jax.random.uniform(ks[15], (L, SHIFT_W), jnp.float32, 0.1, 0.9),
        'w0': jax.random.uniform(ks[16], (L, N_DIR, W_C), jnp.float32, -4.0, 1.0),
        'w2': nrm(17, (L, N_DIR, LORA_W, W_C), 0.1 * LORA_W ** -0.5),
        'a0': nrm(18, (L, N_DIR, W_C), 0.5),
        'a2': nrm(19, (L, N_DIR, LORA_A, W_C), 0.1 * LORA_A ** -0.5),
        'k_k': 0.85 + nrm(20, (L, W_C), 0.05),
        'k_a': 1.0 + nrm(21, (L, W_C), 0.05),
        'r_k': nrm(22, (L, H_C, HEAD_C), 0.1),
        'gn_g': 1.0 + nrm(23, (L, W_C), 0.05),
        'gn_b': nrm(24, (L, W_C), 0.02),
        'w_out': nrm(25, (L, D_MIX, D_MODEL), D_MIX ** -0.5),
    }


def reference(x_prompt, x_sample, c, state_rwkv, c_ctx, ada_w, ada_b, g_pre, g_post, w_in,
              conv_a, conv_b, conv_b_bias, ln_b_g, ln_b_b, mu, w0, w2, a0, a2, k_k, k_a,
              r_k, gn_g, gn_b, w_out):
    xp = x_prompt
    xs = x_sample
    bp = x_prompt.shape[0]
    new_states = []
    for l in range(DEPTH):
        p = {'ada_w': ada_w[l], 'ada_b': ada_b[l], 'g_pre': g_pre[l], 'g_post': g_post[l],
             'w_in': w_in[l], 'conv_a': conv_a[l], 'conv_b': conv_b[l],
             'conv_b_bias': conv_b_bias[l], 'ln_b_g': ln_b_g[l], 'ln_b_b': ln_b_b[l],
             'mu': mu[l], 'w0': w0[l], 'w2': w2[l], 'a0': a0[l], 'a2': a2[l], 'k_k': k_k[l],
             'k_a': k_a[l], 'r_k': r_k[l], 'gn_g': gn_g[l], 'gn_b': gn_b[l], 'w_out': w_out[l]}
        zeros = jnp.zeros((bp, H_C, HEAD_C, HEAD_C), xp.dtype)
        xp, sf, sb = trunk_layer(xp, c_ctx, p, False, zeros, zeros)
        new_states.append(jnp.stack([sf, sb], axis=1))
        xs, _, _ = trunk_layer(xs, c, p, True, state_rwkv[:, l, 0], state_rwkv[:, l, 1])
    new_state_rwkv = jnp.stack(new_states, axis=1)
    return (xp, xs, new_state_rwkv)
```

```python
import functools
import math

import jax
import jax.numpy as jnp
from jax import lax
from jax.experimental import pallas as pl
from jax.experimental.pallas import tpu as pltpu

F32 = jnp.float32
BF16 = jnp.bfloat16

GRID_W = 64
HEAD = 64
HEADS_PER_GROUP = 4
GROUP_W = HEAD * HEADS_PER_GROUP
CHUNK = 64
CONV_B_HALF = 15
LORA = 64
RMS_EPS = 1e-6
LN_EPS = 1e-5
GN_EPS = 64e-5
DECAY_SCALE = math.exp(-0.5)

TOK_TILE = 256
SCAN_BLOCK = 256
VMEM_LIMIT = 56 * 1024 * 1024


def _cparams(sem):
    return pltpu.CompilerParams(dimension_semantics=sem, vmem_limit_bytes=VMEM_LIMIT)


def _dot(a, b):
    return jnp.dot(a.astype(BF16), b.astype(BF16), preferred_element_type=F32)


def _dot_nt(a, b):
    return lax.dot_general(a.astype(BF16), b.astype(BF16), (((1,), (1,)), ((), ())),
                           preferred_element_type=F32)


def _split2(x):
    hi = x.astype(BF16)
    lo = (x - hi.astype(F32)).astype(BF16)
    return hi, lo


def _split3(x):
    h1 = x.astype(BF16)
    r1 = x - h1.astype(F32)
    h2 = r1.astype(BF16)
    h3 = (r1 - h2.astype(F32)).astype(BF16)
    return h1, h2, h3


def _head_sum(x, ones_bd):
    hi, lo = _split2(x)
    return (jnp.dot(hi, ones_bd, preferred_element_type=F32)
            + jnp.dot(lo, ones_bd, preferred_element_type=F32))


def _silu(x):
    return x * jax.nn.sigmoid(x)


def _ada_kernel(mod_ref, w_ref, b_ref, o_ref):
    m = mod_ref[...]
    o_ref[0] = jnp.dot(_silu(m), w_ref[0], preferred_element_type=F32,
                       precision=lax.Precision.HIGHEST) + b_ref[0]


def _ada(mods, ada_w, ada_b):
    n_layers, d, d3 = ada_w.shape
    rows = mods.shape[0]
    blk = d
    return pl.pallas_call(
        _ada_kernel,
        grid=(n_layers, d3 // blk),
        in_specs=[pl.BlockSpec((rows, d), lambda l, j: (0, 0)),
                  pl.BlockSpec((1, d, blk), lambda l, j: (l, 0, j)),
                  pl.BlockSpec((1, 1, blk), lambda l, j: (l, 0, j))],
        out_specs=pl.BlockSpec((1, rows, blk), lambda l, j: (l, 0, j)),
        out_shape=jax.ShapeDtypeStruct((n_layers, rows, d3), F32),
        compiler_params=_cparams(("parallel", "parallel")),
        name="ada",
    )(mods, ada_w, ada_b.reshape(n_layers, 1, d3))


def _inproj_kernel(x_ref, ada_ref, g_ref, w_ref, z_ref, *, d):
    x = x_ref[...]
    ada = ada_ref[0]
    shift, scale = ada[:, 0:d], ada[:, d:2 * d]
    y = x * lax.rsqrt(jnp.mean(x * x, -1, keepdims=True) + RMS_EPS) * g_ref[...]
    h = y * (1.0 + scale) + shift
    z_ref[...] = _dot(h, w_ref[...])


def _inproj(x, ada, g_pre, w_in_bf, seq):
    n, d = x.shape
    d_in = w_in_bf.shape[1]
    per_b = seq // TOK_TILE
    nb = ada.shape[0]
    ada_map = (lambda i: (i // per_b, 0, 0)) if nb > 1 else (lambda i: (0, 0, 0))
    return pl.pallas_call(
        functools.partial(_inproj_kernel, d=d),
        grid=(n // TOK_TILE,),
        in_specs=[pl.BlockSpec((TOK_TILE, d), lambda i: (i, 0)),
                  pl.BlockSpec((1, 1, 3 * d), ada_map),
                  pl.BlockSpec((1, d), lambda i: (0, 0)),
                  pl.BlockSpec((d, d_in), lambda i: (0, 0))],
        out_specs=pl.BlockSpec((TOK_TILE, d_in), lambda i: (i, 0)),
        out_shape=jax.ShapeDtypeStruct((n, d_in), F32),
        compiler_params=_cparams(("parallel",)),
        name="inproj",
    )(x, ada, g_pre.reshape(1, d), w_in_bf)


def _conva_kernel(xa_ref, bg_ref, cg_ref, ga_ref, w_ref, o_ref, *, seg):
    u = cg_ref[...] * xa_ref[...]
    rows = u.shape[0]
    pos = lax.broadcasted_iota(jnp.int32, u.shape, 0) % seg
    prev = jnp.where(pos == 0, 0.0, pltpu.roll(u, 1, 0))
    nxt = jnp.where(pos == seg - 1, 0.0, pltpu.roll(u, rows - 1, 0))
    conv = w_ref[0:1, :] * prev + w_ref[1:2, :] * u + w_ref[2:3, :] * nxt
    o_ref[...] = bg_ref[...] * conv * _silu(ga_ref[...])


def _conva(z, conv_a, seg):
    n = z.shape[0]
    w = conv_a.shape[1]
    col = lambda j: pl.BlockSpec((TOK_TILE, w), lambda i, j=j: (i, j))
    return pl.pallas_call(
        functools.partial(_conva_kernel, seg=seg),
        grid=(n // TOK_TILE,),
        in_specs=[col(0), col(1), col(2), col(3),
                  pl.BlockSpec(conv_a.shape, lambda i: (0, 0))],
        out_specs=pl.BlockSpec((TOK_TILE, w), lambda i: (i, 0)),
        out_shape=jax.ShapeDtypeStruct((n, w), F32),
        compiler_params=_cparams(("parallel",)),
        name="conv_a",
    )(z, z, z, z, conv_a)


def _convb_kernel(u1_ref, u2_ref, gb_ref, w_ref, bias_ref, lg_ref, lb_ref, o_ref, pad_ref,
                  *, seq, stride, pad_rows, row_block):
    width = pad_ref.shape[1]
    zeros = jnp.zeros((pad_rows, width), F32)
    pad_ref[0:pad_rows, :] = zeros
    pad_ref[pad_rows + seq:pad_rows + seq + pad_rows, :] = zeros
    pad_ref[pad_rows:pad_rows + seq, :] = u1_ref[...] * jax.nn.sigmoid(u2_ref[...])
    first = pad_rows - CONV_B_HALF * stride
    n_taps = 2 * CONV_B_HALF + 1

    def block(start):
        acc = jnp.zeros((row_block, width), F32)
        for j in range(n_taps):
            acc = acc + w_ref[j:j + 1, :] * pad_ref[pl.ds(start + first + j * stride, row_block), :]
        u = acc + bias_ref[...]
        mu = jnp.mean(u, -1, keepdims=True)
        var = jnp.mean(jnp.square(u - mu), -1, keepdims=True)
        ln = (u - mu) * lax.rsqrt(var + LN_EPS) * lg_ref[...] + lb_ref[...]
        o_ref[pl.ds(start, row_block), :] = _silu(ln) * _silu(gb_ref[pl.ds(start, row_block), :])

    if stride % 8 == 0:
        def body(i, carry):
            block(pl.multiple_of(i * row_block, row_block))
            return carry
        lax.fori_loop(0, seq // row_block, body, 0)
    else:
        for i in range(seq // row_block):
            block(i * row_block)


def _convb(z, conv_b, bias, ln_g, ln_b, seq, stride):
    n = z.shape[0]
    w = conv_b.shape[1]
    pad_rows = -(-(CONV_B_HALF * stride) // 8) * 8
    col = lambda j: pl.BlockSpec((seq, w), lambda b, j=j: (b, j))
    vec = pl.BlockSpec((1, w), lambda b: (0, 0))
    first_col = 4
    return pl.pallas_call(
        functools.partial(_convb_kernel, seq=seq, stride=stride, pad_rows=pad_rows, row_block=64),
        grid=(n // seq,),
        in_specs=[col(first_col), col(first_col + 1), col(first_col + 2),
                  pl.BlockSpec(conv_b.shape, lambda b: (0, 0)), vec, vec, vec],
        out_specs=pl.BlockSpec((seq, w), lambda b: (b, 0)),
        out_shape=jax.ShapeDtypeStruct((n, w), F32),
        scratch_shapes=[pltpu.VMEM((seq + 2 * pad_rows, w), F32)],
        compiler_params=_cparams(("parallel",)),
        name="conv_b",
    )(z, z, z, conv_b, bias.reshape(1, w), ln_g.reshape(1, w), ln_b.reshape(1, w))


def _prep_kernel(zc_ref, zp_ref, zn_ref, mu_ref, w0_ref, w2_ref, a0_ref, a2_ref, kk_ref, ka_ref,
                 rk_ref, ones_ref,
                 r_o, v_o, kn_o, lw0_o, kd0_o, b0_o, lw1_o, kd1_o, b1_o, bonus_o, *, wc):
    i = pl.program_id(1)
    last = pl.num_programs(1) - 1
    zc = zc_ref[...]
    rows = zc.shape[0]
    prev_row = jnp.where(i == 0, 0.0, zp_ref[7:8, :])
    next_row = jnp.where(i == last, 0.0, zn_ref[0:1, :])
    ridx = lax.broadcasted_iota(jnp.int32, zc.shape, 0)
    up = jnp.where(ridx == 0, prev_row, pltpu.roll(zc, 1, 0))
    dn = jnp.where(ridx == rows - 1, next_row, pltpu.roll(zc, rows - 1, 0))
    zs = zc + mu_ref[...] * (0.5 * (up + dn) - zc)

    r, k, v = zs[:, 0:wc], zs[:, wc:2 * wc], zs[:, 2 * wc:3 * wc]
    lw = jnp.tanh(zs[:, 3 * wc:3 * wc + 2 * LORA])
    la = zs[:, 3 * wc + 2 * LORA:3 * wc + 4 * LORA]
    lane = lax.broadcasted_iota(jnp.int32, lw.shape, 1)
    ones_bd = ones_ref[...]

    kk = k * kk_ref[...]
    kn = kk * lax.rsqrt(_head_sum(kk * kk, ones_bd) + 1e-12)
    r_o[...] = r
    v_o[...] = v
    kn_o[...] = kn

    kd_sum = jnp.zeros_like(k)
    for d, (lw_o, kd_o, b_o) in enumerate(((lw0_o, kd0_o, b0_o), (lw1_o, kd1_o, b1_o))):
        sel = (lane >= d * LORA) & (lane < (d + 1) * LORA)
        wl = w0_ref[d:d + 1, :] + _dot(jnp.where(sel, lw, 0.0), w2_ref[...])
        a = jax.nn.sigmoid(a0_ref[d:d + 1, :] + _dot(jnp.where(sel, la, 0.0), a2_ref[...]))
        kd = k * (1.0 + (a - 1.0) * ka_ref[...])
        lw_o[...] = -DECAY_SCALE * jax.nn.sigmoid(wl)
        kd_o[...] = kd
        b_o[...] = kn * a
        kd_sum = kd_sum + kd
    bonus_o[...] = _head_sum(r * kd_sum * rk_ref[...], ones_bd) * v


def _prep(z, mu, w0, w2, a0, a2, k_k, k_a, r_k, ones_bd, seq, off_c, shift_w):
    n = z.shape[0]
    wc = k_k.shape[0]
    nt = seq // TOK_TILE
    cblk = off_c // shift_w
    assert cblk * shift_w == off_c
    sub = TOK_TILE // 8
    n8 = n // 8
    main = pl.BlockSpec((TOK_TILE, shift_w), lambda b, i: (b * nt + i, cblk))
    prev = pl.BlockSpec((8, shift_w), lambda b, i: (jnp.maximum((b * nt + i) * sub - 1, 0), cblk))
    nxt = pl.BlockSpec((8, shift_w), lambda b, i: (jnp.minimum((b * nt + i + 1) * sub, n8 - 1), cblk))
    full = lambda a: pl.BlockSpec(a.shape, lambda b, i: (0,) * a.ndim)
    out = pl.BlockSpec((TOK_TILE, wc), lambda b, i: (b * nt + i, 0))
    args = (mu.reshape(1, shift_w), w0, w2.reshape(2 * LORA, wc).astype(BF16), a0,
            a2.reshape(2 * LORA, wc).astype(BF16), k_k.reshape(1, wc), k_a.reshape(1, wc),
            r_k.reshape(1, wc), ones_bd)
    return pl.pallas_call(
        functools.partial(_prep_kernel, wc=wc),
        grid=(n // seq, nt),
        in_specs=[main, prev, nxt] + [full(a) for a in args],
        out_specs=[out] * 10,
        out_shape=[jax.ShapeDtypeStruct((n, wc), F32)] * 10,
        compiler_params=_cparams(("parallel", "parallel")),
        name="rwkv_prep",
    )(z, z, z, *args)


def _bd(y, bdmask):
    return jnp.where(bdmask, jnp.concatenate([y] * HEADS_PER_GROUP, axis=0), 0.0)


def _scan_chunk(reverse, r, v, kk, logw, kd, b, st, masks):
    tri, strict, incl, eye, bdmask = masks
    c = r.shape[0]
    h1, h2, h3 = _split3(logw)
    dotf = functools.partial(jnp.dot, preferred_element_type=F32)
    cs = dotf(tri, h1) + dotf(tri, h2) + dotf(tri, h3)
    p = jnp.exp(cs)
    p_prev = jnp.exp(cs - logw)
    p_inv = jnp.exp(-cs)
    p_tot = p[0:1, :] if reverse else p[c - 1:c, :]
    rt = r * p
    at = -(kk * p_prev)
    bt = b * p_inv
    kt = kd * p_inv

    ar = jnp.concatenate([at, rt], axis=0)
    sb = _dot_nt(ar, _bd(bt, bdmask))
    sk = _dot_nt(ar, _bd(kt, bdmask))
    l_ab = jnp.where(strict, sb[0:c], 0.0)
    m_rb = jnp.where(incl, sb[c:2 * c], 0.0)
    m_ak = jnp.where(strict, sk[0:c], 0.0)
    m_rk = jnp.where(incl, sk[c:2 * c], 0.0)

    t_inv = eye + l_ab
    pw = l_ab
    for _ in range(int(math.log2(c)) - 1):
        pw = _dot(pw, _bd(pw, bdmask))
        t_inv = t_inv + _dot(t_inv, _bd(pw, bdmask))

    w1 = _dot(m_ak, _bd(v, bdmask))
    ta = _dot(t_inv, jnp.concatenate([_bd(at, bdmask), _bd(w1, bdmask)], axis=1))
    a_hat, w2 = ta[:, 0:GROUP_W], ta[:, GROUP_W:2 * GROUP_W]
    r_hat = rt + _dot(m_rb, _bd(a_hat, bdmask))
    y0 = _dot(jnp.concatenate([m_rb, m_rk], axis=1),
              jnp.concatenate([_bd(w2, bdmask), _bd(v, bdmask)], axis=0))

    uy = _dot(jnp.concatenate([a_hat, r_hat], axis=0), st) + jnp.concatenate([w2, y0], axis=0)
    u, y = uy[0:c], uy[c:2 * c]
    bk_t = jnp.concatenate([bt * p_tot, kt * p_tot], axis=0).T
    p_col = jnp.broadcast_to(p_tot, (2 * c, GROUP_W)).T
    p_col = jnp.concatenate([p_col, p_col], axis=1)
    st_new = jnp.where(bdmask, st * p_col + _dot(bk_t, jnp.concatenate([u, v], axis=0)), 0.0)
    return y, st_new


def _scan_kernel(*refs, has_s0):
    if has_s0:
        s0_ref, refs = refs[0], refs[1:]
    (rf, vf, kf, lwf, kdf, bf, rb, vb, kb, lwb, kdb, bb, yf_o, yb_o, st_o) = refs
    i = pl.program_id(2)
    c = CHUNK
    n_chunks = rf.shape[0] // c

    row = lax.broadcasted_iota(jnp.int32, (GROUP_W, GROUP_W), 0)
    colm = lax.broadcasted_iota(jnp.int32, (GROUP_W, GROUP_W), 1)
    bdmask = (row // HEAD) == (colm // HEAD)
    t_row = lax.broadcasted_iota(jnp.int32, (c, GROUP_W), 0)
    t_col = lax.broadcasted_iota(jnp.int32, (c, GROUP_W), 1) % c
    eye = (t_row == t_col).astype(F32)
    tr = lax.broadcasted_iota(jnp.int32, (c, c), 0)
    tc = lax.broadcasted_iota(jnp.int32, (c, c), 1)
    masks_f = ((tc <= tr).astype(BF16), t_col < t_row, t_col <= t_row, eye, bdmask)
    masks_b = ((tc >= tr).astype(BF16), t_col > t_row, t_col >= t_row, eye, bdmask)

    @pl.when(i == 0)
    def _():
        if has_s0:
            st_o[0, 0, 0] = jnp.where(bdmask, s0_ref[0, 0, 0], 0.0)
            st_o[0, 0, 1] = jnp.where(bdmask, s0_ref[0, 0, 1], 0.0)
        else:
            st_o[...] = jnp.zeros(st_o.shape, F32)

    def body(j, carry):
        sf = pl.ds(pl.multiple_of(j * c, c), c)
        sb = pl.ds(pl.multiple_of((n_chunks - 1 - j) * c, c), c)
        y, st = _scan_chunk(False, rf[sf, :], vf[sf, :], kf[sf, :], lwf[sf, :], kdf[sf, :], bf[sf, :],
                            st_o[0, 0, 0], masks_f)
        yf_o[sf, :] = y
        st_o[0, 0, 0] = st
        y, st = _scan_chunk(True, rb[sb, :], vb[sb, :], kb[sb, :], lwb[sb, :], kdb[sb, :], bb[sb, :],
                            st_o[0, 0, 1], masks_b)
        yb_o[sb, :] = y
        st_o[0, 0, 1] = st
        return carry

    lax.fori_loop(0, n_chunks, body, 0)


def _scan(prep, s0, seq):
    r, v, kn, lw0, kd0, b0, lw1, kd1, b1 = prep
    n, wc = r.shape
    nb = n // seq
    ng = wc // GROUP_W
    nt = seq // SCAN_BLOCK
    fwd = pl.BlockSpec((SCAN_BLOCK, GROUP_W), lambda b, g, i: (b * nt + i, g))
    bwd = pl.BlockSpec((SCAN_BLOCK, GROUP_W), lambda b, g, i: (b * nt + nt - 1 - i, g))
    st_spec = pl.BlockSpec((1, 1, 2, GROUP_W, GROUP_W), lambda b, g, i: (b, g, 0, 0, 0))
    ins = [r, v, kn, lw0, kd0, b0, r, v, kn, lw1, kd1, b1]
    in_specs = [fwd] * 6 + [bwd] * 6
    if s0 is not None:
        ins = [s0] + ins
        in_specs = [st_spec] + in_specs
    return pl.pallas_call(
        functools.partial(_scan_kernel, has_s0=s0 is not None),
        grid=(nb, ng, nt),
        in_specs=in_specs,
        out_specs=[fwd, bwd, st_spec],
        out_shape=[jax.ShapeDtypeStruct((n, wc), F32), jax.ShapeDtypeStruct((n, wc), F32),
                   jax.ShapeDtypeStruct((nb, ng, 2, GROUP_W, GROUP_W), F32)],
        compiler_params=_cparams(("parallel", "parallel", "arbitrary")),
        name="rwkv_scan",
    )(*ins)


def _outproj_kernel(x_ref, ada_ref, ya_ref, yb_ref, yf_ref, ybw_ref, bonus_ref, gc_ref,
                    gng_ref, gnb_ref, ones_ref, w_ref, gp_ref, o_ref, *, d):
    ones_bd = ones_ref[...]
    ys = yf_ref[...] + ybw_ref[...]
    mu = _head_sum(ys, ones_bd) * (1.0 / HEAD)
    dev = ys - mu
    var = _head_sum(dev * dev, ones_bd) * (1.0 / HEAD)
    yg = dev * lax.rsqrt(var + GN_EPS) * gng_ref[...] + gnb_ref[...]
    yc = (yg + bonus_ref[...]) * _silu(gc_ref[...])
    cat = jnp.concatenate([ya_ref[...], yb_ref[...], yc], axis=-1)
    out = _dot(cat, w_ref[...])
    gate = ada_ref[0][:, 2 * d:3 * d]
    nrm = out * lax.rsqrt(jnp.mean(out * out, -1, keepdims=True) + RMS_EPS) * gp_ref[...]
    o_ref[...] = x_ref[...] + gate * nrm


def _outproj(x, ada, ya, yb, yf, ybw, bonus, z, gn_g, gn_b, ones_bd, w_out_bf, g_post, seq, off_g):
    n, d = x.shape
    wc = yf.shape[1]
    wa = ya.shape[1]
    per_b = seq // TOK_TILE
    nb = ada.shape[0]
    ada_map = (lambda i: (i // per_b, 0, 0)) if nb > 1 else (lambda i: (0, 0, 0))
    gblk = off_g // wc
    assert gblk * wc == off_g
    tok = lambda w: pl.BlockSpec((TOK_TILE, w), lambda i: (i, 0))
    full = lambda a: pl.BlockSpec(a.shape, lambda i: (0,) * a.ndim)
    small = (gn_g.reshape(1, wc), gn_b.reshape(1, wc), ones_bd, w_out_bf, g_post.reshape(1, d))
    return pl.pallas_call(
        functools.partial(_outproj_kernel, d=d),
        grid=(n // TOK_TILE,),
        in_specs=[tok(d), pl.BlockSpec((1, 1, 3 * d), ada_map), tok(wa), tok(wa), tok(wc), tok(wc),
                  tok(wc), pl.BlockSpec((TOK_TILE, wc), lambda i: (i, gblk))]
                 + [full(a) for a in small],
        out_specs=tok(d),
        out_shape=jax.ShapeDtypeStruct((n, d), F32),
        compiler_params=_cparams(("parallel",)),
        name="outproj",
    )(x, ada, ya, yb, yf, ybw, bonus, z, *small)


def _trunk_layer(x, ada, p, s0, seq, latent):
    wa = p['conv_a'].shape[1]
    wb = p['conv_b'].shape[1]
    wc = p['k_k'].shape[0]
    shift_w = p['mu'].shape[0]
    off_c = 4 * wa + 3 * wb
    off_g = off_c + shift_w
    z = _inproj(x, ada, p['g_pre'], p['w_in'], seq)
    ya = _conva(z, p['conv_a'], GRID_W if latent else seq)
    yb = _convb(z, p['conv_b'], p['conv_b_bias'], p['ln_b_g'], p['ln_b_b'], seq,
                GRID_W if latent else 1)
    prep = _prep(z, p['mu'], p['w0'], p['w2'], p['a0'], p['a2'], p['k_k'], p['k_a'], p['r_k'],
                 p['ones_bd'], seq, off_c, shift_w)
    yf, ybw, st = _scan(prep[:9], s0, seq)
    x = _outproj(x, ada, ya, yb, yf, ybw, prep[9], z, p['gn_g'], p['gn_b'], p['ones_bd'],
                 p['w_out'], p['g_post'], seq, off_g)
    return x, st


def _pack_states(s):
    b, nd, h, hv, hk = s.shape
    g = h // HEADS_PER_GROUP
    st = jnp.swapaxes(s, -1, -2).reshape(b, nd, g, HEADS_PER_GROUP * hk, hv)
    st = jnp.tile(st, (1, 1, 1, 1, HEADS_PER_GROUP))
    return jnp.swapaxes(st, 1, 2)


def _unpack_states(st):
    b, g, nd = st.shape[:3]
    blocks = st.reshape(b, g, nd, HEADS_PER_GROUP, HEAD, HEADS_PER_GROUP, HEAD)
    diag = jnp.stack([blocks[:, :, :, h, :, h, :] for h in range(HEADS_PER_GROUP)], axis=3)
    diag = jnp.swapaxes(diag, -1, -2)
    return jnp.swapaxes(diag, 1, 2).reshape(b, nd, g * HEADS_PER_GROUP, HEAD, HEAD)


def kernel(x_prompt, x_sample, c, state_rwkv, c_ctx, ada_w, ada_b, g_pre, g_post, w_in, conv_a,
           conv_b, conv_b_bias, ln_b_g, ln_b_b, mu, w0, w2, a0, a2, k_k, k_a, r_k, gn_g, gn_b, w_out):
    bp, seq_p, d = x_prompt.shape
    bs, seq_s, _ = x_sample.shape
    depth = ada_w.shape[0]
    wc = k_k.shape[1]

    rows = -(-(1 + bs) // 8) * 8
    mods = jnp.zeros((rows, d), F32).at[0].set(c_ctx).at[1:1 + bs].set(c)
    ada = _ada(mods, ada_w, ada_b)
    head_id = jnp.arange(wc) // HEAD
    ones_bd = (head_id[:, None] == head_id[None, :]).astype(BF16)
    w_in_bf = w_in.astype(BF16)
    w_out_bf = w_out.astype(BF16)

    xp = x_prompt.reshape(bp * seq_p, d)
    xs = x_sample.reshape(bs * seq_s, d)
    new_states = []
    for l in range(depth):
        p = {'g_pre': g_pre[l], 'g_post': g_post[l], 'w_in': w_in_bf[l], 'conv_a': conv_a[l],
             'conv_b': conv_b[l], 'conv_b_bias': conv_b_bias[l], 'ln_b_g': ln_b_g[l],
             'ln_b_b': ln_b_b[l], 'mu': mu[l], 'w0': w0[l], 'w2': w2[l], 'a0': a0[l], 'a2': a2[l],
             'k_k': k_k[l], 'k_a': k_a[l], 'r_k': r_k[l].reshape(-1), 'gn_g': gn_g[l],
             'gn_b': gn_b[l], 'w_out': w_out_bf[l], 'ones_bd': ones_bd}
        xp, st = _trunk_layer(xp, ada[l, 0:1].reshape(1, 1, 3 * d), p, None, seq_p, False)
        new_states.append(_unpack_states(st))
        xs, _ = _trunk_layer(xs, ada[l, 1:1 + bs].reshape(bs, 1, 3 * d), p,
                             _pack_states(state_rwkv[:, l]), seq_s, True)
    return (xp.reshape(bp, seq_p, d), xs.reshape(bs, seq_s, d), jnp.stack(new_states, axis=1))
```

```python
import functools
import math

import jax
import jax.numpy as jnp
from jax import lax
from jax.experimental import pallas as pl
from jax.experimental.pallas import tpu as pltpu

F32 = jnp.float32
BF16 = jnp.bfloat16

GRID_W = 64
HEAD = 64
HEADS_PER_GROUP = 4
GROUP_W = HEAD * HEADS_PER_GROUP
CHUNK = 64
CONV_B_HALF = 15
LORA = 64
RMS_EPS = 1e-6
LN_EPS = 1e-5
GN_EPS = 64e-5
DECAY_SCALE = math.exp(-0.5)

TOK_TILE = 256
SCAN_BLOCK = 256
VMEM_LIMIT = 56 * 1024 * 1024


def _cparams(sem):
    return pltpu.CompilerParams(dimension_semantics=sem, vmem_limit_bytes=VMEM_LIMIT)


def _dot(a, b):
    return jnp.dot(a.astype(BF16), b.astype(BF16), preferred_element_type=F32)


def _dot_nt(a, b):
    return lax.dot_general(a.astype(BF16), b.astype(BF16), (((1,), (1,)), ((), ())),
                           preferred_element_type=F32)


def _split2(x):
    hi = x.astype(BF16)
    lo = (x - hi.astype(F32)).astype(BF16)
    return hi, lo


def _split3(x):
    h1 = x.astype(BF16)
    r1 = x - h1.astype(F32)
    h2 = r1.astype(BF16)
    h3 = (r1 - h2.astype(F32)).astype(BF16)
    return h1, h2, h3


def _head_sum(x, ones_bd):
    hi, lo = _split2(x)
    return (jnp.dot(hi, ones_bd, preferred_element_type=F32)
            + jnp.dot(lo, ones_bd, preferred_element_type=F32))


def _silu(x):
    return x * jax.nn.sigmoid(x)


def _ada_kernel(mod_ref, w_ref, b_ref, o_ref):
    m = mod_ref[...]
    o_ref[0] = jnp.dot(_silu(m), w_ref[0], preferred_element_type=F32,
                       precision=lax.Precision.HIGHEST) + b_ref[0]


def _ada(mods, ada_w, ada_b):
    n_layers, d, d3 = ada_w.shape
    rows = mods.shape[0]
    blk = d
    return pl.pallas_call(
        _ada_kernel,
        grid=(n_layers, d3 // blk),
        in_specs=[pl.BlockSpec((rows, d), lambda l, j: (0, 0)),
                  pl.BlockSpec((1, d, blk), lambda l, j: (l, 0, j)),
                  pl.BlockSpec((1, 1, blk), lambda l, j: (l, 0, j))],
        out_specs=pl.BlockSpec((1, rows, blk), lambda l, j: (l, 0, j)),
        out_shape=jax.ShapeDtypeStruct((n_layers, rows, d3), F32),
        compiler_params=_cparams(("parallel", "parallel")),
        name="ada",
    )(mods, ada_w, ada_b.reshape(n_layers, 1, d3))


def _inproj_kernel(x_ref, ada_ref, g_ref, w_ref, z_ref, *, d):
    x = x_ref[...]
    ada = ada_ref[0]
    shift, scale = ada[:, 0:d], ada[:, d:2 * d]
    y = x * lax.rsqrt(jnp.mean(x * x, -1, keepdims=True) + RMS_EPS) * g_ref[...]
    h = y * (1.0 + scale) + shift
    z_ref[...] = _dot(h, w_ref[...])


def _inproj(x, ada, g_pre, w_in_bf, seq):
    n, d = x.shape
    d_in = w_in_bf.shape[1]
    per_b = seq // TOK_TILE
    nb = ada.shape[0]
    ada_map = (lambda i: (i // per_b, 0, 0)) if nb > 1 else (lambda i: (0, 0, 0))
    return pl.pallas_call(
        functools.partial(_inproj_kernel, d=d),
        grid=(n // TOK_TILE,),
        in_specs=[pl.BlockSpec((TOK_TILE, d), lambda i: (i, 0)),
                  pl.BlockSpec((1, 1, 3 * d), ada_map),
                  pl.BlockSpec((1, d), lambda i: (0, 0)),
                  pl.BlockSpec((d, d_in), lambda i: (0, 0))],
        out_specs=pl.BlockSpec((TOK_TILE, d_in), lambda i: (i, 0)),
        out_shape=jax.ShapeDtypeStruct((n, d_in), F32),
        compiler_params=_cparams(("parallel",)),
        name="inproj",
    )(x, ada, g_pre.reshape(1, d), w_in_bf)


def _conva_kernel(xa_ref, bg_ref, cg_ref, ga_ref, w_ref, o_ref, *, seg):
    u = cg_ref[...] * xa_ref[...]
    rows = u.shape[0]
    pos = lax.broadcasted_iota(jnp.int32, u.shape, 0) % seg
    prev = jnp.where(pos == 0, 0.0, pltpu.roll(u, 1, 0))
    nxt = jnp.where(pos == seg - 1, 0.0, pltpu.roll(u, rows - 1, 0))
    conv = w_ref[0:1, :] * prev + w_ref[1:2, :] * u + w_ref[2:3, :] * nxt
    o_ref[...] = bg_ref[...] * conv * _silu(ga_ref[...])


def _conva(z, conv_a, seg):
    n = z.shape[0]
    w = conv_a.shape[1]
    col = lambda j: pl.BlockSpec((TOK_TILE, w), lambda i, j=j: (i, j))
    return pl.pallas_call(
        functools.partial(_conva_kernel, seg=seg),
        grid=(n // TOK_TILE,),
        in_specs=[col(0), col(1), col(2), col(3),
                  pl.BlockSpec(conv_a.shape, lambda i: (0, 0))],
        out_specs=pl.BlockSpec((TOK_TILE, w), lambda i: (i, 0)),
        out_shape=jax.ShapeDtypeStruct((n, w), F32),
        compiler_params=_cparams(("parallel",)),
        name="conv_a",
    )(z, z, z, z, conv_a)


def _convb_kernel(u1_ref, u2_ref, gb_ref, w_ref, bias_ref, lg_ref, lb_ref, o_ref, pad_ref,
                  *, seq, stride, pad_rows, row_block):
    width = pad_ref.shape[1]
    zeros = jnp.zeros((pad_rows, width), F32)
    pad_ref[0:pad_rows, :] = zeros
    pad_ref[pad_rows + seq:pad_rows + seq + pad_rows, :] = zeros
    pad_ref[pad_rows:pad_rows + seq, :] = u1_ref[...] * jax.nn.sigmoid(u2_ref[...])
    first = pad_rows - CONV_B_HALF * stride
    n_taps = 2 * CONV_B_HALF + 1

    def block(start):
        acc = jnp.zeros((row_block, width), F32)
        for j in range(n_taps):
            acc = acc + w_ref[j:j + 1, :] * pad_ref[pl.ds(start + first + j * stride, row_block), :]
        u = acc + bias_ref[...]
        mu = jnp.mean(u, -1, keepdims=True)
        var = jnp.mean(jnp.square(u - mu), -1, keepdims=True)
        ln = (u - mu) * lax.rsqrt(var + LN_EPS) * lg_ref[...] + lb_ref[...]
        o_ref[pl.ds(start, row_block), :] = _silu(ln) * _silu(gb_ref[pl.ds(start, row_block), :])

    if stride % 8 == 0:
        def body(i, carry):
            block(pl.multiple_of(i * row_block, row_block))
            return carry
        lax.fori_loop(0, seq // row_block, body, 0)
    else:
        for i in range(seq // row_block):
            block(i * row_block)


def _convb(z, conv_b, bias, ln_g, ln_b, seq, stride):
    n = z.shape[0]
    w = conv_b.shape[1]
    pad_rows = -(-(CONV_B_HALF * stride) // 8) * 8
    col = lambda j: pl.BlockSpec((seq, w), lambda b, j=j: (b, j))
    vec = pl.BlockSpec((1, w), lambda b: (0, 0))
    first_col = 4
    return pl.pallas_call(
        functools.partial(_convb_kernel, seq=seq, stride=stride, pad_rows=pad_rows, row_block=64),
        grid=(n // seq,),
        in_specs=[col(first_col), col(first_col + 1), col(first_col + 2),
                  pl.BlockSpec(conv_b.shape, lambda b: (0, 0)), vec, vec, vec],
        out_specs=pl.BlockSpec((seq, w), lambda b: (b, 0)),
        out_shape=jax.ShapeDtypeStruct((n, w), F32),
        scratch_shapes=[pltpu.VMEM((seq + 2 * pad_rows, w), F32)],
        compiler_params=_cparams(("parallel",)),
        name="conv_b",
    )(z, z, z, conv_b, bias.reshape(1, w), ln_g.reshape(1, w), ln_b.reshape(1, w))


def _prep_kernel(zc_ref, zp_ref, zn_ref, mu_ref, w0_ref, w2_ref, a0_ref, a2_ref, kk_ref, ka_ref,
                 rk_ref, ones_ref,
                 r_o, v_o, kn_o, lw0_o, kd0_o, b0_o, lw1_o, kd1_o, b1_o, bonus_o, *, wc):
    i = pl.program_id(1)
    last = pl.num_programs(1) - 1
    zc = zc_ref[...]
    rows = zc.shape[0]
    prev_row = jnp.where(i == 0, 0.0, zp_ref[7:8, :])
    next_row = jnp.where(i == last, 0.0, zn_ref[0:1, :])
    ridx = lax.broadcasted_iota(jnp.int32, zc.shape, 0)
    up = jnp.where(ridx == 0, prev_row, pltpu.roll(zc, 1, 0))
    dn = jnp.where(ridx == rows - 1, next_row, pltpu.roll(zc, rows - 1, 0))
    zs = zc + mu_ref[...] * (0.5 * (up + dn) - zc)

    r, k, v = zs[:, 0:wc], zs[:, wc:2 * wc], zs[:, 2 * wc:3 * wc]
    lw = jnp.tanh(zs[:, 3 * wc:3 * wc + 2 * LORA])
    la = zs[:, 3 * wc + 2 * LORA:3 * wc + 4 * LORA]
    lane = lax.broadcasted_iota(jnp.int32, lw.shape, 1)
    ones_bd = ones_ref[...]

    kk = k * kk_ref[...]
    kn = kk * lax.rsqrt(_head_sum(kk * kk, ones_bd) + 1e-12)
    r_o[...] = r
    v_o[...] = v
    kn_o[...] = kn

    kd_sum = jnp.zeros_like(k)
    for d, (lw_o, kd_o, b_o) in enumerate(((lw0_o, kd0_o, b0_o), (lw1_o, kd1_o, b1_o))):
        sel = (lane >= d * LORA) & (lane < (d + 1) * LORA)
        wl = w0_ref[d:d + 1, :] + _dot(jnp.where(sel, lw, 0.0), w2_ref[...])
        a = jax.nn.sigmoid(a0_ref[d:d + 1, :] + _dot(jnp.where(sel, la, 0.0), a2_ref[...]))
        kd = k * (1.0 + (a - 1.0) * ka_ref[...])
        lw_o[...] = -DECAY_SCALE * jax.nn.sigmoid(wl)
        kd_o[...] = kd
        b_o[...] = kn * a
        kd_sum = kd_sum + kd
    bonus_o[...] = _head_sum(r * kd_sum * rk_ref[...], ones_bd) * v


def _prep(z, mu, w0, w2, a0, a2, k_k, k_a, r_k, ones_bd, seq, off_c, shift_w):
    n = z.shape[0]
    wc = k_k.shape[0]
    nt = seq // TOK_TILE
    cblk = off_c // shift_w
    assert cblk * shift_w == off_c
    sub = TOK_TILE // 8
    n8 = n // 8
    main = pl.BlockSpec((TOK_TILE, shift_w), lambda b, i: (b * nt + i, cblk))
    prev = pl.BlockSpec((8, shift_w), lambda b, i: (jnp.maximum((b * nt + i) * sub - 1, 0), cblk))
    nxt = pl.BlockSpec((8, shift_w), lambda b, i: (jnp.minimum((b * nt + i + 1) * sub, n8 - 1), cblk))
    full = lambda a: pl.BlockSpec(a.shape, lambda b, i: (0,) * a.ndim)
    out = pl.BlockSpec((TOK_TILE, wc), lambda b, i: (b * nt + i, 0))
    args = (mu.reshape(1, shift_w), w0, w2.reshape(2 * LORA, wc).astype(BF16), a0,
            a2.reshape(2 * LORA, wc).astype(BF16), k_k.reshape(1, wc), k_a.reshape(1, wc),
            r_k.reshape(1, wc), ones_bd)
    return pl.pallas_call(
        functools.partial(_prep_kernel, wc=wc),
        grid=(n // seq, nt),
        in_specs=[main, prev, nxt] + [full(a) for a in args],
        out_specs=[out] * 10,
        out_shape=[jax.ShapeDtypeStruct((n, wc), F32)] * 10,
        compiler_params=_cparams(("parallel", "parallel")),
        name="rwkv_prep",
    )(z, z, z, *args)


def _bd(y, bdmask):
    return jnp.where(bdmask, jnp.concatenate([y] * HEADS_PER_GROUP, axis=0), 0.0)


def _chunk_operands(reverse, r, v, kk, logw, kd, b, masks):
    tri, strict, incl, eye, bdmask = masks
    c = r.shape[0]
    h1, h2, h3 = _split3(logw)
    dotf = functools.partial(jnp.dot, preferred_element_type=F32)
    cs = dotf(tri, h1) + dotf(tri, h2) + dotf(tri, h3)
    yield
    p = jnp.exp(cs)
    p_prev = jnp.exp(cs - logw)
    p_inv = jnp.exp(-cs)
    p_tot = p[0:1, :] if reverse else p[c - 1:c, :]
    rt = r * p
    at = -(kk * p_prev)
    bt = b * p_inv
    kt = kd * p_inv

    ar = jnp.concatenate([at, rt], axis=0)
    sb = _dot_nt(ar, _bd(bt, bdmask))
    sk = _dot_nt(ar, _bd(kt, bdmask))
    yield
    l_ab = jnp.where(strict, sb[0:c], 0.0)
    m_rb = jnp.where(incl, sb[c:2 * c], 0.0)
    m_ak = jnp.where(strict, sk[0:c], 0.0)
    m_rk = jnp.where(incl, sk[c:2 * c], 0.0)

    t_inv = eye + l_ab
    pw = l_ab
    for _ in range(int(math.log2(c)) - 1):
        pw = _dot(pw, _bd(pw, bdmask))
        yield
        t_inv = t_inv + _dot(t_inv, _bd(pw, bdmask))
        yield

    v_bd = _bd(v, bdmask).astype(BF16)
    w1 = _dot(m_ak, v_bd)
    yield
    ta = _dot(t_inv, jnp.concatenate([_bd(at, bdmask), _bd(w1, bdmask)], axis=1))
    a_hat, w2 = ta[:, 0:GROUP_W], ta[:, GROUP_W:2 * GROUP_W]
    yield
    r_hat = rt + _dot(m_rb, _bd(a_hat, bdmask))
    y0 = _dot(jnp.concatenate([m_rb, m_rk], axis=1),
              jnp.concatenate([_bd(w2, bdmask).astype(BF16), v_bd], axis=0))
    ar_hat = jnp.concatenate([a_hat, r_hat], axis=0).astype(BF16)
    wy = jnp.concatenate([w2, y0], axis=0)
    bk_t = jnp.concatenate([bt * p_tot, kt * p_tot], axis=0).T.astype(BF16)
    p_col = jnp.broadcast_to(p_tot, (2 * c, GROUP_W)).T
    p_col = jnp.concatenate([p_col, p_col], axis=1)
    return ar_hat, wy, bk_t, p_col, v


def _interleave(tasks):
    results = [None] * len(tasks)
    live = list(range(len(tasks)))
    while live:
        for k in list(live):
            try:
                next(tasks[k])
            except StopIteration as done:
                results[k] = done.value
                live.remove(k)
    return results


def _state_chain(st, operands, y_ref, row_slices, lanes, bdmask):
    for (ar_hat, wy, bk_t, p_col, v), rows in zip(operands, row_slices):
        c = v.shape[0]
        uy = _dot(ar_hat, st) + wy
        yield
        u, y = uy[0:c], uy[c:2 * c]
        y_ref[rows, lanes] = y
        st = jnp.where(bdmask, st * p_col + _dot(bk_t, jnp.concatenate([u, v], axis=0)), 0.0)
        yield
    return st


def _scan_kernel(*refs, has_s0):
    if has_s0:
        s0_ref, refs = refs[0], refs[1:]
    (rf, vf, kf, lwf, kdf, bf, rb, vb, kb, lwb, kdb, bb, yf_o, yb_o, st_o) = refs
    i = pl.program_id(1)
    c = CHUNK
    n_chunks = rf.shape[0] // c
    n_groups = rf.shape[1] // GROUP_W

    row = lax.broadcasted_iota(jnp.int32, (GROUP_W, GROUP_W), 0)
    colm = lax.broadcasted_iota(jnp.int32, (GROUP_W, GROUP_W), 1)
    bdmask = (row // HEAD) == (colm // HEAD)
    t_row = lax.broadcasted_iota(jnp.int32, (c, GROUP_W), 0)
    t_col = lax.broadcasted_iota(jnp.int32, (c, GROUP_W), 1) % c
    eye = (t_row == t_col).astype(F32)
    tr = lax.broadcasted_iota(jnp.int32, (c, c), 0)
    tc = lax.broadcasted_iota(jnp.int32, (c, c), 1)
    masks_f = ((tc <= tr).astype(BF16), t_col < t_row, t_col <= t_row, eye, bdmask)
    masks_b = ((tc >= tr).astype(BF16), t_col > t_row, t_col >= t_row, eye, bdmask)

    @pl.when(i == 0)
    def _():
        if has_s0:
            for g in range(n_groups):
                for d in range(2):
                    st_o[0, g, d] = jnp.where(bdmask, s0_ref[0, g, d], 0.0)
        else:
            st_o[...] = jnp.zeros(st_o.shape, F32)

    rows = [pl.ds(j * c, c) for j in range(n_chunks)]
    lanes = [pl.ds(g * GROUP_W, GROUP_W) for g in range(n_groups)]
    fwd_refs = (rf, vf, kf, lwf, kdf, bf)
    bwd_refs = (rb, vb, kb, lwb, kdb, bb)
    ops = _interleave(
        [_chunk_operands(False, *(ref[s, l] for ref in fwd_refs), masks_f) for l in lanes for s in rows]
        + [_chunk_operands(True, *(ref[s, l] for ref in bwd_refs), masks_b) for l in lanes for s in rows])
    chains = []
    for g in range(n_groups):
        of = ops[g * n_chunks:(g + 1) * n_chunks]
        ob = ops[(n_groups + g) * n_chunks:(n_groups + g + 1) * n_chunks]
        chains.append(_state_chain(st_o[0, g, 0], of, yf_o, rows, lanes[g], bdmask))
        chains.append(_state_chain(st_o[0, g, 1], ob[::-1], yb_o, rows[::-1], lanes[g], bdmask))
    finals = _interleave(chains)
    for g in range(n_groups):
        st_o[0, g, 0] = finals[2 * g]
        st_o[0, g, 1] = finals[2 * g + 1]


def _scan(prep, s0, seq):
    r, v, kn, lw0, kd0, b0, lw1, kd1, b1 = prep
    n, wc = r.shape
    nb = n // seq
    ng = wc // GROUP_W
    nt = seq // SCAN_BLOCK
    fwd = pl.BlockSpec((SCAN_BLOCK, wc), lambda b, i: (b * nt + i, 0))
    bwd = pl.BlockSpec((SCAN_BLOCK, wc), lambda b, i: (b * nt + nt - 1 - i, 0))
    st_spec = pl.BlockSpec((1, ng, 2, GROUP_W, GROUP_W), lambda b, i: (b, 0, 0, 0, 0))
    ins = [r, v, kn, lw0, kd0, b0, r, v, kn, lw1, kd1, b1]
    in_specs = [fwd] * 6 + [bwd] * 6
    if s0 is not None:
        ins = [s0] + ins
        in_specs = [st_spec] + in_specs
    return pl.pallas_call(
        functools.partial(_scan_kernel, has_s0=s0 is not None),
        grid=(nb, nt),
        in_specs=in_specs,
        out_specs=[fwd, bwd, st_spec],
        out_shape=[jax.ShapeDtypeStruct((n, wc), F32), jax.ShapeDtypeStruct((n, wc), F32),
                   jax.ShapeDtypeStruct((nb, ng, 2, GROUP_W, GROUP_W), F32)],
        compiler_params=_cparams(("parallel", "arbitrary")),
        name="rwkv_scan",
    )(*ins)


def _outproj_kernel(x_ref, ada_ref, ya_ref, yb_ref, yf_ref, ybw_ref, bonus_ref, gc_ref,
                    gng_ref, gnb_ref, ones_ref, w_ref, gp_ref, o_ref, *, d):
    ones_bd = ones_ref[...]
    ys = yf_ref[...] + ybw_ref[...]
    mu = _head_sum(ys, ones_bd) * (1.0 / HEAD)
    dev = ys - mu
    var = _head_sum(dev * dev, ones_bd) * (1.0 / HEAD)
    yg = dev * lax.rsqrt(var + GN_EPS) * gng_ref[...] + gnb_ref[...]
    yc = (yg + bonus_ref[...]) * _silu(gc_ref[...])
    cat = jnp.concatenate([ya_ref[...], yb_ref[...], yc], axis=-1)
    out = _dot(cat, w_ref[...])
    gate = ada_ref[0][:, 2 * d:3 * d]
    nrm = out * lax.rsqrt(jnp.mean(out * out, -1, keepdims=True) + RMS_EPS) * gp_ref[...]
    o_ref[...] = x_ref[...] + gate * nrm


def _outproj(x, ada, ya, yb, yf, ybw, bonus, z, gn_g, gn_b, ones_bd, w_out_bf, g_post, seq, off_g):
    n, d = x.shape
    wc = yf.shape[1]
    wa = ya.shape[1]
    per_b = seq // TOK_TILE
    nb = ada.shape[0]
    ada_map = (lambda i: (i // per_b, 0, 0)) if nb > 1 else (lambda i: (0, 0, 0))
    gblk = off_g // wc
    assert gblk * wc == off_g
    tok = lambda w: pl.BlockSpec((TOK_TILE, w), lambda i: (i, 0))
    full = lambda a: pl.BlockSpec(a.shape, lambda i: (0,) * a.ndim)
    small = (gn_g.reshape(1, wc), gn_b.reshape(1, wc), ones_bd, w_out_bf, g_post.reshape(1, d))
    return pl.pallas_call(
        functools.partial(_outproj_kernel, d=d),
        grid=(n // TOK_TILE,),
        in_specs=[tok(d), pl.BlockSpec((1, 1, 3 * d), ada_map), tok(wa), tok(wa), tok(wc), tok(wc),
                  tok(wc), pl.BlockSpec((TOK_TILE, wc), lambda i: (i, gblk))]
                 + [full(a) for a in small],
        out_specs=tok(d),
        out_shape=jax.ShapeDtypeStruct((n, d), F32),
        compiler_params=_cparams(("parallel",)),
        name="outproj",
    )(x, ada, ya, yb, yf, ybw, bonus, z, *small)


def _trunk_layer(x, ada, p, s0, seq, latent):
    wa = p['conv_a'].shape[1]
    wb = p['conv_b'].shape[1]
    wc = p['k_k'].shape[0]
    shift_w = p['mu'].shape[0]
    off_c = 4 * wa + 3 * wb
    off_g = off_c + shift_w
    z = _inproj(x, ada, p['g_pre'], p['w_in'], seq)
    ya = _conva(z, p['conv_a'], GRID_W if latent else seq)
    yb = _convb(z, p['conv_b'], p['conv_b_bias'], p['ln_b_g'], p['ln_b_b'], seq,
                GRID_W if latent else 1)
    prep = _prep(z, p['mu'], p['w0'], p['w2'], p['a0'], p['a2'], p['k_k'], p['k_a'], p['r_k'],
                 p['ones_bd'], seq, off_c, shift_w)
    yf, ybw, st = _scan(prep[:9], s0, seq)
    x = _outproj(x, ada, ya, yb, yf, ybw, prep[9], z, p['gn_g'], p['gn_b'], p['ones_bd'],
                 p['w_out'], p['g_post'], seq, off_g)
    return x, st


def _pack_states(s):
    b, nd, h, hv, hk = s.shape
    g = h // HEADS_PER_GROUP
    st = jnp.swapaxes(s, -1, -2).reshape(b, nd, g, HEADS_PER_GROUP * hk, hv)
    st = jnp.tile(st, (1, 1, 1, 1, HEADS_PER_GROUP))
    return jnp.swapaxes(st, 1, 2)


def _unpack_states(st):
    b, g, nd = st.shape[:3]
    blocks = st.reshape(b, g, nd, HEADS_PER_GROUP, HEAD, HEADS_PER_GROUP, HEAD)
    diag = jnp.stack([blocks[:, :, :, h, :, h, :] for h in range(HEADS_PER_GROUP)], axis=3)
    diag = jnp.swapaxes(diag, -1, -2)
    return jnp.swapaxes(diag, 1, 2).reshape(b, nd, g * HEADS_PER_GROUP, HEAD, HEAD)


def kernel(x_prompt, x_sample, c, state_rwkv, c_ctx, ada_w, ada_b, g_pre, g_post, w_in, conv_a,
           conv_b, conv_b_bias, ln_b_g, ln_b_b, mu, w0, w2, a0, a2, k_k, k_a, r_k, gn_g, gn_b, w_out):
    bp, seq_p, d = x_prompt.shape
    bs, seq_s, _ = x_sample.shape
    depth = ada_w.shape[0]
    wc = k_k.shape[1]

    rows = -(-(1 + bs) // 8) * 8
    mods = jnp.zeros((rows, d), F32).at[0].set(c_ctx).at[1:1 + bs].set(c)
    ada = _ada(mods, ada_w, ada_b)
    head_id = jnp.arange(wc) // HEAD
    ones_bd = (head_id[:, None] == head_id[None, :]).astype(BF16)
    w_in_bf = w_in.astype(BF16)
    w_out_bf = w_out.astype(BF16)

    xp = x_prompt.reshape(bp * seq_p, d)
    xs = x_sample.reshape(bs * seq_s, d)
    new_states = []
    for l in range(depth):
        p = {'g_pre': g_pre[l], 'g_post': g_post[l], 'w_in': w_in_bf[l], 'conv_a': conv_a[l],
             'conv_b': conv_b[l], 'conv_b_bias': conv_b_bias[l], 'ln_b_g': ln_b_g[l],
             'ln_b_b': ln_b_b[l], 'mu': mu[l], 'w0': w0[l], 'w2': w2[l], 'a0': a0[l], 'a2': a2[l],
             'k_k': k_k[l], 'k_a': k_a[l], 'r_k': r_k[l].reshape(-1), 'gn_g': gn_g[l],
             'gn_b': gn_b[l], 'w_out': w_out_bf[l], 'ones_bd': ones_bd}
        xp, st = _trunk_layer(xp, ada[l, 0:1].reshape(1, 1, 3 * d), p, None, seq_p, False)
        new_states.append(_unpack_states(st))
        xs, _ = _trunk_layer(xs, ada[l, 1:1 + bs].reshape(bs, 1, 3 * d), p,
                             _pack_states(state_rwkv[:, l]), seq_s, True)
    return (xp.reshape(bp, seq_p, d), xs.reshape(bs, seq_s, d), jnp.stack(new_states, axis=1))
```

```python
import functools
import math

import jax
import jax.numpy as jnp
from jax import lax
from jax.experimental import pallas as pl
from jax.experimental.pallas import tpu as pltpu

F32 = jnp.float32
BF16 = jnp.bfloat16

SUBLANES = 8
GRID_W = 64
HEAD = 64
HEADS_PER_GROUP = 4
GROUP_W = HEAD * HEADS_PER_GROUP
CHUNK = 64
CONV_B_HALF = 15
LORA = 64
RMS_EPS = 1e-6
LN_EPS = 1e-5
GN_EPS = 64e-5
DECAY_SCALE = math.exp(-0.5)

TOK_TILE = 256
SCAN_BLOCK = 256
VMEM_LIMIT = 56 * 1024 * 1024


def _cparams(sem):
    return pltpu.CompilerParams(dimension_semantics=sem, vmem_limit_bytes=VMEM_LIMIT)


def _dot(a, b):
    return jnp.dot(a.astype(BF16), b.astype(BF16), preferred_element_type=F32)


def _dot_nt(a, b):
    return lax.dot_general(a.astype(BF16), b.astype(BF16), (((1,), (1,)), ((), ())),
                           preferred_element_type=F32)


def _split3(x):
    h1 = x.astype(BF16)
    r1 = x - h1.astype(F32)
    h2 = r1.astype(BF16)
    h3 = (r1 - h2.astype(F32)).astype(BF16)
    return h1, h2, h3


def _head_sum(x, ones_g):
    parts = [_dot(x[:, g * GROUP_W:(g + 1) * GROUP_W], ones_g) for g in range(x.shape[1] // GROUP_W)]
    return jnp.concatenate(parts, axis=1)


def _silu(x):
    return x * jax.nn.sigmoid(x)


def _ada_kernel(mod_ref, w_ref, b_ref, o_ref):
    m = mod_ref[...]
    o_ref[0] = jnp.dot(_silu(m), w_ref[0], preferred_element_type=F32,
                       precision=lax.Precision.HIGHEST) + b_ref[0]


def _ada(mods, ada_w, ada_b):
    n_layers, d, d3 = ada_w.shape
    rows = mods.shape[0]
    blk = d
    return pl.pallas_call(
        _ada_kernel,
        grid=(n_layers, d3 // blk),
        in_specs=[pl.BlockSpec((rows, d), lambda l, j: (0, 0)),
                  pl.BlockSpec((1, d, blk), lambda l, j: (l, 0, j)),
                  pl.BlockSpec((1, 1, blk), lambda l, j: (l, 0, j))],
        out_specs=pl.BlockSpec((1, rows, blk), lambda l, j: (l, 0, j)),
        out_shape=jax.ShapeDtypeStruct((n_layers, rows, d3), F32),
        compiler_params=_cparams(("parallel", "parallel")),
        name="ada",
    )(mods, ada_w, ada_b.reshape(n_layers, 1, d3))


def _inproj_kernel(x_ref, ada_ref, g_ref, w_ref, ca_ref, ya_ref, ub_ref, sgb_ref, zc_ref, sgc_ref,
                   *, d, wa, wb, shift_w, seg):
    x = x_ref[...]
    ada = ada_ref[0]
    shift, scale = ada[:, 0:d], ada[:, d:2 * d]
    y = x * lax.rsqrt(jnp.mean(x * x, -1, keepdims=True) + RMS_EPS) * g_ref[...]
    h = (y * (1.0 + scale) + shift).astype(BF16)
    proj = lambda lo, hi: jnp.dot(h, w_ref[:, lo:hi], preferred_element_type=F32)

    za = proj(0, 4 * wa)
    xa, bg, cg, ga = (za[:, k * wa:(k + 1) * wa] for k in range(4))
    u = cg * xa
    rows = u.shape[0]
    pos = lax.broadcasted_iota(jnp.int32, u.shape, 0) % seg
    prev = jnp.where(pos == 0, 0.0, pltpu.roll(u, 1, 0))
    nxt = jnp.where(pos == seg - 1, 0.0, pltpu.roll(u, rows - 1, 0))
    conv = ca_ref[0:1, :] * prev + ca_ref[1:2, :] * u + ca_ref[2:3, :] * nxt
    ya_ref[...] = bg * conv * _silu(ga)

    off_b = 4 * wa
    zb = proj(off_b, off_b + 3 * wb)
    ub_ref[...] = zb[:, 0:wb] * jax.nn.sigmoid(zb[:, wb:2 * wb])
    sgb_ref[...] = _silu(zb[:, 2 * wb:3 * wb])

    off_c = off_b + 3 * wb
    zc_ref[...] = proj(off_c, off_c + shift_w)
    sgc_ref[...] = _silu(proj(off_c + shift_w, w_ref.shape[1]))


def _inproj(x, ada, g_pre, w_in_bf, conv_a, wb, shift_w, seq, seg):
    n, d = x.shape
    d_in = w_in_bf.shape[1]
    wa = conv_a.shape[1]
    wc = d_in - 4 * wa - 3 * wb - shift_w
    assert seq % TOK_TILE == 0 and TOK_TILE % seg == 0
    per_b = seq // TOK_TILE
    nb = ada.shape[0]
    ada_map = (lambda i: (i // per_b, 0, 0)) if nb > 1 else (lambda i: (0, 0, 0))
    tok = lambda w: pl.BlockSpec((TOK_TILE, w), lambda i: (i, 0))
    widths = (wa, wb, wb, shift_w, wc)
    return pl.pallas_call(
        functools.partial(_inproj_kernel, d=d, wa=wa, wb=wb, shift_w=shift_w, seg=seg),
        grid=(n // TOK_TILE,),
        in_specs=[tok(d),
                  pl.BlockSpec((1, 1, 3 * d), ada_map),
                  pl.BlockSpec((1, d), lambda i: (0, 0)),
                  pl.BlockSpec((d, d_in), lambda i: (0, 0)),
                  pl.BlockSpec(conv_a.shape, lambda i: (0, 0))],
        out_specs=[tok(w) for w in widths],
        out_shape=[jax.ShapeDtypeStruct((n, w), F32) for w in widths],
        compiler_params=_cparams(("parallel",)),
        name="inproj",
    )(x, ada, g_pre.reshape(1, d), w_in_bf, conv_a)


def _convb_finish(u, sgb, lg_ref, lb_ref):
    mu = jnp.mean(u, -1, keepdims=True)
    var = jnp.mean(jnp.square(u - mu), -1, keepdims=True)
    ln = (u - mu) * lax.rsqrt(var + LN_EPS) * lg_ref[...] + lb_ref[...]
    return _silu(ln) * sgb


def _convb_grid_kernel(u_ref, sgb_ref, w_ref, bias_ref, lg_ref, lb_ref, o_ref, *, seq):
    n_rows = seq // GRID_W
    for r in range(n_rows):
        accs = [None, None]
        for k, q in enumerate(range(max(0, r - CONV_B_HALF), min(n_rows - 1, r + CONV_B_HALF) + 1)):
            j = q - r + CONV_B_HALF
            term = w_ref[j:j + 1, :] * u_ref[q * GRID_W:(q + 1) * GRID_W, :]
            accs[k % 2] = term if accs[k % 2] is None else accs[k % 2] + term
        u = accs[0] + accs[1] + bias_ref[...]
        out = slice(r * GRID_W, (r + 1) * GRID_W)
        o_ref[out, :] = _convb_finish(u, sgb_ref[out, :], lg_ref, lb_ref)


def _convb_seq_kernel(u_ref, sgb_ref, w_ref, bias_ref, lg_ref, lb_ref, o_ref, pad_ref,
                      *, seq, pad_rows, row_block):
    width = pad_ref.shape[1]
    zeros = jnp.zeros((pad_rows, width), F32)
    pad_ref[0:pad_rows, :] = zeros
    pad_ref[pad_rows + seq:pad_rows + seq + pad_rows, :] = zeros
    pad_ref[pad_rows:pad_rows + seq, :] = u_ref[...]
    first = pad_rows - CONV_B_HALF
    for i in range(seq // row_block):
        start = i * row_block
        accs = [None, None]
        for j in range(2 * CONV_B_HALF + 1):
            term = w_ref[j:j + 1, :] * pad_ref[start + first + j:start + first + j + row_block, :]
            accs[j % 2] = term if accs[j % 2] is None else accs[j % 2] + term
        u = accs[0] + accs[1] + bias_ref[...]
        out = slice(start, start + row_block)
        o_ref[out, :] = _convb_finish(u, sgb_ref[out, :], lg_ref, lb_ref)


def _convb(ub, sgb, conv_b, bias, ln_g, ln_b, seq, latent):
    n, w = ub.shape
    blk = pl.BlockSpec((seq, w), lambda b: (b, 0))
    vec = pl.BlockSpec((1, w), lambda b: (0, 0))
    if latent:
        body = functools.partial(_convb_grid_kernel, seq=seq)
        scratch = []
    else:
        pad_rows = -(-CONV_B_HALF // SUBLANES) * SUBLANES
        body = functools.partial(_convb_seq_kernel, seq=seq, pad_rows=pad_rows, row_block=64)
        scratch = [pltpu.VMEM((seq + 2 * pad_rows, w), F32)]
    return pl.pallas_call(
        body,
        grid=(n // seq,),
        in_specs=[blk, blk, pl.BlockSpec(conv_b.shape, lambda b: (0, 0)), vec, vec, vec],
        out_specs=blk,
        out_shape=jax.ShapeDtypeStruct((n, w), F32),
        scratch_shapes=scratch,
        compiler_params=_cparams(("parallel",)),
        name="conv_b",
    )(ub, sgb, conv_b, bias.reshape(1, w), ln_g.reshape(1, w), ln_b.reshape(1, w))


def _prep_kernel(zc_ref, zp_ref, zn_ref, mu_ref, w0_ref, w2_ref, a0_ref, a2_ref, kk_ref, ka_ref,
                 rk_ref, ones_ref,
                 r_o, v_o, kn_o, lw0_o, kd0_o, b0_o, lw1_o, kd1_o, b1_o, bonus_o, *, wc):
    i = pl.program_id(1)
    last = pl.num_programs(1) - 1
    zc = zc_ref[...]
    rows = zc.shape[0]
    prev_row = jnp.where(i == 0, 0.0, zp_ref[SUBLANES - 1:SUBLANES, :])
    next_row = jnp.where(i == last, 0.0, zn_ref[0:1, :])
    ridx = lax.broadcasted_iota(jnp.int32, zc.shape, 0)
    up = jnp.where(ridx == 0, prev_row, pltpu.roll(zc, 1, 0))
    dn = jnp.where(ridx == rows - 1, next_row, pltpu.roll(zc, rows - 1, 0))
    zs = zc + mu_ref[...] * (0.5 * (up + dn) - zc)

    r, k, v = zs[:, 0:wc], zs[:, wc:2 * wc], zs[:, 2 * wc:3 * wc]
    lw = jnp.tanh(zs[:, 3 * wc:3 * wc + 2 * LORA])
    la = zs[:, 3 * wc + 2 * LORA:3 * wc + 4 * LORA]
    lane = lax.broadcasted_iota(jnp.int32, lw.shape, 1)
    ones_g = ones_ref[...]

    kk = k * kk_ref[...]
    kn = kk * lax.rsqrt(_head_sum(kk * kk, ones_g) + 1e-12)
    r_o[...] = r
    v_o[...] = v
    kn_o[...] = kn

    kd_sum = jnp.zeros_like(k)
    for d, (lw_o, kd_o, b_o) in enumerate(((lw0_o, kd0_o, b0_o), (lw1_o, kd1_o, b1_o))):
        sel = (lane >= d * LORA) & (lane < (d + 1) * LORA)
        wl = w0_ref[d:d + 1, :] + _dot(jnp.where(sel, lw, 0.0), w2_ref[...])
        a = jax.nn.sigmoid(a0_ref[d:d + 1, :] + _dot(jnp.where(sel, la, 0.0), a2_ref[...]))
        kd = k * (1.0 + (a - 1.0) * ka_ref[...])
        lw_o[...] = -DECAY_SCALE * jax.nn.sigmoid(wl)
        kd_o[...] = kd
        b_o[...] = kn * a
        kd_sum = kd_sum + kd
    bonus_o[...] = _head_sum(r * kd_sum * rk_ref[...], ones_g) * v


def _prep(zc, mu, w0, w2, a0, a2, k_k, k_a, r_k, ones_g, seq):
    n, shift_w = zc.shape
    wc = k_k.shape[0]
    nt = seq // TOK_TILE
    sub = TOK_TILE // SUBLANES
    n_sub = n // SUBLANES
    main = pl.BlockSpec((TOK_TILE, shift_w), lambda b, i: (b * nt + i, 0))
    prev = pl.BlockSpec((SUBLANES, shift_w), lambda b, i: (jnp.maximum((b * nt + i) * sub - 1, 0), 0))
    nxt = pl.BlockSpec((SUBLANES, shift_w),
                       lambda b, i: (jnp.minimum((b * nt + i + 1) * sub, n_sub - 1), 0))
    full = lambda a: pl.BlockSpec(a.shape, lambda b, i: (0,) * a.ndim)
    out = pl.BlockSpec((TOK_TILE, wc), lambda b, i: (b * nt + i, 0))
    args = (mu.reshape(1, shift_w), w0, w2.reshape(2 * LORA, wc).astype(BF16), a0,
            a2.reshape(2 * LORA, wc).astype(BF16), k_k.reshape(1, wc), k_a.reshape(1, wc),
            r_k.reshape(1, wc), ones_g)
    return pl.pallas_call(
        functools.partial(_prep_kernel, wc=wc),
        grid=(n // seq, nt),
        in_specs=[main, prev, nxt] + [full(a) for a in args],
        out_specs=[out] * 10,
        out_shape=[jax.ShapeDtypeStruct((n, wc), F32)] * 10,
        compiler_params=_cparams(("parallel", "parallel")),
        name="rwkv_prep",
    )(zc, zc, zc, *args)


def _bd(y, bdmask):
    return jnp.where(bdmask, jnp.concatenate([y] * HEADS_PER_GROUP, axis=0), 0.0).astype(BF16)


def _chunk_operands(reverse, r, v, kk, logw, kd, b, masks):
    tri3, strict, incl, eye, bdmask = masks
    c = r.shape[0]
    cs = jnp.dot(tri3, jnp.concatenate(_split3(logw), axis=0), preferred_element_type=F32)
    yield
    p = jnp.exp(cs)
    p_prev = jnp.exp(cs - logw)
    p_inv = jnp.exp(-cs)
    p_tot = p[0:1, :] if reverse else p[c - 1:c, :]
    rt = r * p
    at = -(kk * p_prev)
    bt = b * p_inv
    kt = kd * p_inv

    ar = jnp.concatenate([at, rt], axis=0)
    sb = _dot_nt(ar, _bd(bt, bdmask))
    sk = _dot_nt(ar, _bd(kt, bdmask))
    yield
    l_ab = jnp.where(strict, sb[0:c], 0.0)
    m_rb = jnp.where(incl, sb[c:2 * c], 0.0)
    m_ak = jnp.where(strict, sk[0:c], 0.0)
    m_rk = jnp.where(incl, sk[c:2 * c], 0.0)

    wv = _dot(jnp.concatenate([m_ak, m_rk], axis=0), _bd(v, bdmask))

    pw = l_ab
    out = _dot(jnp.concatenate([pw, m_rb], axis=0), _bd(pw, bdmask))
    yield
    t_inv = eye + l_ab
    mt = m_rb + out[c:2 * c]
    pw = out[0:c]
    n_steps = int(math.log2(c))
    for step in range(1, n_steps):
        lhs = [t_inv, mt] if step == n_steps - 1 else [t_inv, mt, pw]
        out = _dot(jnp.concatenate(lhs, axis=0), _bd(pw, bdmask))
        yield
        t_inv = t_inv + out[0:c]
        mt = mt + out[c:2 * c]
        pw = out[2 * c:3 * c] if step < n_steps - 1 else None

    tx = _dot(jnp.concatenate([t_inv, mt], axis=0),
              jnp.concatenate([_bd(at, bdmask), _bd(wv[0:c], bdmask)], axis=1))
    yield
    a_hat, w2 = tx[0:c, 0:GROUP_W], tx[0:c, GROUP_W:2 * GROUP_W]
    r_hat = rt + tx[c:2 * c, 0:GROUP_W]
    y0 = tx[c:2 * c, GROUP_W:2 * GROUP_W] + wv[c:2 * c]
    ar_hat = jnp.concatenate([a_hat, r_hat], axis=0).astype(BF16)
    wy = jnp.concatenate([w2, y0], axis=0)
    bk_t = jnp.concatenate([bt * p_tot, kt * p_tot], axis=0).T.astype(BF16)
    p_col = jnp.broadcast_to(p_tot, (2 * c, GROUP_W)).T
    p_col = jnp.concatenate([p_col, p_col], axis=1)
    return ar_hat, wy, bk_t, p_col, v


def _interleave(tasks):
    results = [None] * len(tasks)
    live = list(range(len(tasks)))
    while live:
        for k in list(live):
            try:
                next(tasks[k])
            except StopIteration as done:
                results[k] = done.value
                live.remove(k)
    return results


def _state_chain(st, operands, y_ref, row_slices, lanes, bdmask):
    for (ar_hat, wy, bk_t, p_col, v), rows in zip(operands, row_slices):
        c = v.shape[0]
        uy = _dot(ar_hat, st) + wy
        yield
        u, y = uy[0:c], uy[c:2 * c]
        y_ref[rows, lanes] = y
        st = jnp.where(bdmask, st * p_col + _dot(bk_t, jnp.concatenate([u, v], axis=0)), 0.0)
        yield
    return st


def _scan_kernel(*refs, has_s0):
    if has_s0:
        s0_ref, refs = refs[0], refs[1:]
    (rf, vf, kf, lwf, kdf, bf, rb, vb, kb, lwb, kdb, bb, yf_o, yb_o, st_o) = refs
    i = pl.program_id(1)
    c = CHUNK
    n_chunks = rf.shape[0] // c
    n_groups = rf.shape[1] // GROUP_W

    row = lax.broadcasted_iota(jnp.int32, (GROUP_W, GROUP_W), 0)
    colm = lax.broadcasted_iota(jnp.int32, (GROUP_W, GROUP_W), 1)
    bdmask = (row // HEAD) == (colm // HEAD)
    t_row = lax.broadcasted_iota(jnp.int32, (c, GROUP_W), 0)
    t_col = lax.broadcasted_iota(jnp.int32, (c, GROUP_W), 1) % c
    eye = (t_row == t_col).astype(F32)
    tr = lax.broadcasted_iota(jnp.int32, (c, 3 * c), 0)
    tc = lax.broadcasted_iota(jnp.int32, (c, 3 * c), 1) % c
    masks_f = ((tc <= tr).astype(BF16), t_col < t_row, t_col <= t_row, eye, bdmask)
    masks_b = ((tc >= tr).astype(BF16), t_col > t_row, t_col >= t_row, eye, bdmask)

    @pl.when(i == 0)
    def _():
        if has_s0:
            for g in range(n_groups):
                for d in range(2):
                    st_o[0, g, d] = jnp.where(bdmask, s0_ref[0, g, d], 0.0)
        else:
            st_o[...] = jnp.zeros(st_o.shape, F32)

    rows = [pl.ds(j * c, c) for j in range(n_chunks)]
    lanes = [pl.ds(g * GROUP_W, GROUP_W) for g in range(n_groups)]
    fwd_refs = (rf, vf, kf, lwf, kdf, bf)
    bwd_refs = (rb, vb, kb, lwb, kdb, bb)
    ops = _interleave(
        [_chunk_operands(False, *(ref[s, l] for ref in fwd_refs), masks_f) for l in lanes for s in rows]
        + [_chunk_operands(True, *(ref[s, l] for ref in bwd_refs), masks_b) for l in lanes for s in rows])
    chains = []
    for g in range(n_groups):
        of = ops[g * n_chunks:(g + 1) * n_chunks]
        ob = ops[(n_groups + g) * n_chunks:(n_groups + g + 1) * n_chunks]
        chains.append(_state_chain(st_o[0, g, 0], of, yf_o, rows, lanes[g], bdmask))
        chains.append(_state_chain(st_o[0, g, 1], ob[::-1], yb_o, rows[::-1], lanes[g], bdmask))
    finals = _interleave(chains)
    for g in range(n_groups):
        st_o[0, g, 0] = finals[2 * g]
        st_o[0, g, 1] = finals[2 * g + 1]


def _scan(prep, s0, seq, emit_state):
    r, v, kn, lw0, kd0, b0, lw1, kd1, b1 = prep
    n, wc = r.shape
    nb = n // seq
    ng = wc // GROUP_W
    nt = seq // SCAN_BLOCK
    fwd = pl.BlockSpec((SCAN_BLOCK, wc), lambda b, i: (b * nt + i, 0))
    bwd = pl.BlockSpec((SCAN_BLOCK, wc), lambda b, i: (b * nt + nt - 1 - i, 0))
    st_shape = (1, ng, 2, GROUP_W, GROUP_W)
    st_spec = pl.BlockSpec(st_shape, lambda b, i: (b, 0, 0, 0, 0))
    ins = [r, v, kn, lw0, kd0, b0, r, v, kn, lw1, kd1, b1]
    in_specs = [fwd] * 6 + [bwd] * 6
    if s0 is not None:
        ins = [s0] + ins
        in_specs = [st_spec] + in_specs
    y_shape = jax.ShapeDtypeStruct((n, wc), F32)
    out_specs, out_shape, scratch = [fwd, bwd], [y_shape, y_shape], []
    if emit_state:
        out_specs.append(st_spec)
        out_shape.append(jax.ShapeDtypeStruct((nb,) + st_shape[1:], F32))
    else:
        scratch.append(pltpu.VMEM(st_shape, F32))
    outs = pl.pallas_call(
        functools.partial(_scan_kernel, has_s0=s0 is not None),
        grid=(nb, nt),
        in_specs=in_specs,
        out_specs=out_specs,
        out_shape=out_shape,
        scratch_shapes=scratch,
        compiler_params=_cparams(("parallel", "arbitrary")),
        name="rwkv_scan",
    )(*ins)
    return (outs[0], outs[1], outs[2] if emit_state else None)


def _outproj_kernel(x_ref, ada_ref, ya_ref, yb_ref, yf_ref, ybw_ref, bonus_ref, sgc_ref,
                    gng_ref, gnb_ref, ones_ref, w_ref, gp_ref, o_ref, *, d):
    ones_g = ones_ref[...]
    ys = yf_ref[...] + ybw_ref[...]
    mu = _head_sum(ys, ones_g) * (1.0 / HEAD)
    dev = ys - mu
    var = _head_sum(dev * dev, ones_g) * (1.0 / HEAD)
    yg = dev * lax.rsqrt(var + GN_EPS) * gng_ref[...] + gnb_ref[...]
    yc = (yg + bonus_ref[...]) * sgc_ref[...]
    cat = jnp.concatenate([ya_ref[...], yb_ref[...], yc], axis=-1)
    out = _dot(cat, w_ref[...])
    gate = ada_ref[0][:, 2 * d:3 * d]
    nrm = out * lax.rsqrt(jnp.mean(out * out, -1, keepdims=True) + RMS_EPS) * gp_ref[...]
    o_ref[...] = x_ref[...] + gate * nrm


def _outproj(x, ada, ya, yb, yf, ybw, bonus, sgc, gn_g, gn_b, ones_g, w_out_bf, g_post, seq):
    n, d = x.shape
    wc = yf.shape[1]
    wa = ya.shape[1]
    per_b = seq // TOK_TILE
    nb = ada.shape[0]
    ada_map = (lambda i: (i // per_b, 0, 0)) if nb > 1 else (lambda i: (0, 0, 0))
    tok = lambda w: pl.BlockSpec((TOK_TILE, w), lambda i: (i, 0))
    full = lambda a: pl.BlockSpec(a.shape, lambda i: (0,) * a.ndim)
    small = (gn_g.reshape(1, wc), gn_b.reshape(1, wc), ones_g, w_out_bf, g_post.reshape(1, d))
    return pl.pallas_call(
        functools.partial(_outproj_kernel, d=d),
        grid=(n // TOK_TILE,),
        in_specs=[tok(d), pl.BlockSpec((1, 1, 3 * d), ada_map), tok(wa), tok(wa), tok(wc), tok(wc),
                  tok(wc), tok(wc)] + [full(a) for a in small],
        out_specs=tok(d),
        out_shape=jax.ShapeDtypeStruct((n, d), F32),
        compiler_params=_cparams(("parallel",)),
        name="outproj",
    )(x, ada, ya, yb, yf, ybw, bonus, sgc, *small)


def _trunk_layer(x, ada, p, s0, seq, latent):
    wb = p['conv_b'].shape[1]
    shift_w = p['mu'].shape[0]
    ya, ub, sgb, zc, sgc = _inproj(x, ada, p['g_pre'], p['w_in'], p['conv_a'], wb, shift_w, seq,
                                   GRID_W if latent else seq)
    yb = _convb(ub, sgb, p['conv_b'], p['conv_b_bias'], p['ln_b_g'], p['ln_b_b'], seq, latent)
    prep = _prep(zc, p['mu'], p['w0'], p['w2'], p['a0'], p['a2'], p['k_k'], p['k_a'], p['r_k'],
                 p['ones_g'], seq)
    yf, ybw, st = _scan(prep[:9], s0, seq, emit_state=not latent)
    x = _outproj(x, ada, ya, yb, yf, ybw, prep[9], sgc, p['gn_g'], p['gn_b'], p['ones_g'],
                 p['w_out'], p['g_post'], seq)
    return x, st


def _pack_states(s):
    b, nd, h, hv, hk = s.shape
    g = h // HEADS_PER_GROUP
    st = jnp.swapaxes(s, -1, -2).reshape(b, nd, g, HEADS_PER_GROUP * hk, hv)
    st = jnp.tile(st, (1, 1, 1, 1, HEADS_PER_GROUP))
    return jnp.swapaxes(st, 1, 2)


def _unpack_states(st):
    b, g, nd = st.shape[:3]
    diag = jnp.stack([st[:, :, :, h * HEAD:(h + 1) * HEAD, h * HEAD:(h + 1) * HEAD]
                      for h in range(HEADS_PER_GROUP)], axis=3)
    diag = jnp.swapaxes(diag, -1, -2)
    return jnp.swapaxes(diag, 1, 2).reshape(b, nd, g * HEADS_PER_GROUP, HEAD, HEAD)


def kernel(x_prompt, x_sample, c, state_rwkv, c_ctx, ada_w, ada_b, g_pre, g_post, w_in, conv_a,
           conv_b, conv_b_bias, ln_b_g, ln_b_b, mu, w0, w2, a0, a2, k_k, k_a, r_k, gn_g, gn_b, w_out):
    bp, seq_p, d = x_prompt.shape
    bs, seq_s, _ = x_sample.shape
    depth = ada_w.shape[0]

    rows = -(-(1 + bs) // SUBLANES) * SUBLANES
    mods = jnp.zeros((rows, d), F32).at[0].set(c_ctx).at[1:1 + bs].set(c)
    ada = _ada(mods, ada_w, ada_b)
    head_id = jnp.arange(GROUP_W) // HEAD
    ones_g = (head_id[:, None] == head_id[None, :]).astype(BF16)
    w_in_bf = w_in.astype(BF16)
    w_out_bf = w_out.astype(BF16)

    xp = x_prompt.reshape(bp * seq_p, d)
    xs = x_sample.reshape(bs * seq_s, d)
    new_states = []
    for l in range(depth):
        p = {'g_pre': g_pre[l], 'g_post': g_post[l], 'w_in': w_in_bf[l], 'conv_a': conv_a[l],
             'conv_b': conv_b[l], 'conv_b_bias': conv_b_bias[l], 'ln_b_g': ln_b_g[l],
             'ln_b_b': ln_b_b[l], 'mu': mu[l], 'w0': w0[l], 'w2': w2[l], 'a0': a0[l], 'a2': a2[l],
             'k_k': k_k[l], 'k_a': k_a[l], 'r_k': r_k[l].reshape(-1), 'gn_g': gn_g[l],
             'gn_b': gn_b[l], 'w_out': w_out_bf[l], 'ones_g': ones_g}
        xp, st = _trunk_layer(xp, ada[l, 0:1].reshape(1, 1, 3 * d), p, None, seq_p, False)
        new_states.append(_unpack_states(st))
        xs, _ = _trunk_layer(xs, ada[l, 1:1 + bs].reshape(bs, 1, 3 * d), p,
                             _pack_states(state_rwkv[:, l]), seq_s, True)
    return (xp.reshape(bp, seq_p, d), xs.reshape(bs, seq_s, d), jnp.stack(new_states, axis=1))
```

```python
import functools
import math

import jax
import jax.numpy as jnp
from jax import lax
from jax.experimental import pallas as pl
from jax.experimental.pallas import tpu as pltpu

F32 = jnp.float32
BF16 = jnp.bfloat16

SUBLANES = 8
GRID_W = 64
HEAD = 64
HEADS_PER_GROUP = 4
GROUP_W = HEAD * HEADS_PER_GROUP
CHUNK = 64
CONV_B_HALF = 15
LORA = 64
RMS_EPS = 1e-6
LN_EPS = 1e-5
GN_EPS = 64e-5
DECAY_SCALE = math.exp(-0.5)

TOK_TILE = 256
SCAN_BLOCK = 256
VMEM_LIMIT = 56 * 1024 * 1024


def _cparams(sem):
    return pltpu.CompilerParams(dimension_semantics=sem, vmem_limit_bytes=VMEM_LIMIT)


def _dot(a, b):
    return jnp.dot(a.astype(BF16), b.astype(BF16), preferred_element_type=F32)


def _dot_nt(a, b):
    return lax.dot_general(a.astype(BF16), b.astype(BF16), (((1,), (1,)), ((), ())),
                           preferred_element_type=F32)


def _split3(x):
    h1 = x.astype(BF16)
    r1 = x - h1.astype(F32)
    h2 = r1.astype(BF16)
    h3 = (r1 - h2.astype(F32)).astype(BF16)
    return h1, h2, h3


def _head_sum(x, ones_g):
    parts = [_dot(x[:, g * GROUP_W:(g + 1) * GROUP_W], ones_g) for g in range(x.shape[1] // GROUP_W)]
    return jnp.concatenate(parts, axis=1)


def _silu(x):
    return x * jax.nn.sigmoid(x)


def _ada_kernel(mod_ref, w_ref, b_ref, o_ref):
    m = mod_ref[...]
    o_ref[0] = jnp.dot(_silu(m), w_ref[0], preferred_element_type=F32,
                       precision=lax.Precision.HIGHEST) + b_ref[0]


def _ada(mods, ada_w, ada_b):
    n_layers, d, d3 = ada_w.shape
    rows = mods.shape[0]
    blk = d
    return pl.pallas_call(
        _ada_kernel,
        grid=(n_layers, d3 // blk),
        in_specs=[pl.BlockSpec((rows, d), lambda l, j: (0, 0)),
                  pl.BlockSpec((1, d, blk), lambda l, j: (l, 0, j)),
                  pl.BlockSpec((1, 1, blk), lambda l, j: (l, 0, j))],
        out_specs=pl.BlockSpec((1, rows, blk), lambda l, j: (l, 0, j)),
        out_shape=jax.ShapeDtypeStruct((n_layers, rows, d3), F32),
        compiler_params=_cparams(("parallel", "parallel")),
        name="ada",
    )(mods, ada_w, ada_b.reshape(n_layers, 1, d3))


def _rwkv_operands(zc, prev_row, next_row, mu_ref, w0_ref, w2_ref, a0_ref, a2_ref, kk_ref, ka_ref,
                   rk_ref, ones_ref, outs, wc):
    r_o, v_o, kn_o, lw0_o, kd0_o, b0_o, lw1_o, kd1_o, b1_o, bonus_o = outs
    rows = zc.shape[0]
    ridx = lax.broadcasted_iota(jnp.int32, zc.shape, 0)
    up = jnp.where(ridx == 0, prev_row, pltpu.roll(zc, 1, 0))
    dn = jnp.where(ridx == rows - 1, next_row, pltpu.roll(zc, rows - 1, 0))
    zs = zc + mu_ref[...] * (0.5 * (up + dn) - zc)

    r, k, v = zs[:, 0:wc], zs[:, wc:2 * wc], zs[:, 2 * wc:3 * wc]
    lw = jnp.tanh(zs[:, 3 * wc:3 * wc + 2 * LORA])
    la = zs[:, 3 * wc + 2 * LORA:3 * wc + 4 * LORA]
    lane = lax.broadcasted_iota(jnp.int32, lw.shape, 1)
    ones_g = ones_ref[...]

    kk = k * kk_ref[...]
    kn = kk * lax.rsqrt(_head_sum(kk * kk, ones_g) + 1e-12)
    r_o[...] = r
    v_o[...] = v.astype(v_o.dtype)
    kn_o[...] = kn

    kd_sum = jnp.zeros_like(k)
    for d, (lw_o, kd_o, b_o) in enumerate(((lw0_o, kd0_o, b0_o), (lw1_o, kd1_o, b1_o))):
        sel = (lane >= d * LORA) & (lane < (d + 1) * LORA)
        wl = w0_ref[d:d + 1, :] + _dot(jnp.where(sel, lw, 0.0), w2_ref[...])
        a = jax.nn.sigmoid(a0_ref[d:d + 1, :] + _dot(jnp.where(sel, la, 0.0), a2_ref[...]))
        kd = k * (1.0 + (a - 1.0) * ka_ref[...])
        lw_o[...] = -DECAY_SCALE * jax.nn.sigmoid(wl)
        kd_o[...] = kd
        b_o[...] = kn * a
        kd_sum = kd_sum + kd
    bonus_o[...] = _head_sum(r * kd_sum * rk_ref[...], ones_g) * v


def _inproj_kernel(x_ref, xp_ref, xn_ref, ada_ref, g_ref, w_ref, ca_ref, mu_ref, w0_ref, w2_ref,
                   a0_ref, a2_ref, kk_ref, ka_ref, rk_ref, ones_ref,
                   ya_ref, ub_ref, sgb_ref, sgc_ref, *c_outs, d, wa, wb, shift_w, seg, per_b):
    ada = ada_ref[0]
    shift, scale = ada[:, 0:d], ada[:, d:2 * d]

    def mod_norm(x):
        y = x * lax.rsqrt(jnp.mean(x * x, -1, keepdims=True) + RMS_EPS) * g_ref[...]
        return y * (1.0 + scale) + shift

    h32 = mod_norm(x_ref[...])
    h = h32.astype(BF16)
    proj = lambda lo, hi: jnp.dot(h, w_ref[:, lo:hi], preferred_element_type=F32)

    za = proj(0, 4 * wa)
    xa, bg, cg, ga = (za[:, k * wa:(k + 1) * wa] for k in range(4))
    u = cg * xa
    rows = u.shape[0]
    pos = lax.broadcasted_iota(jnp.int32, u.shape, 0) % seg
    prev = jnp.where(pos == 0, 0.0, pltpu.roll(u, 1, 0))
    nxt = jnp.where(pos == seg - 1, 0.0, pltpu.roll(u, rows - 1, 0))
    conv = ca_ref[0:1, :] * prev + ca_ref[1:2, :] * u + ca_ref[2:3, :] * nxt
    ya_ref[...] = (bg * conv * _silu(ga)).astype(ya_ref.dtype)

    off_b = 4 * wa
    zb = proj(off_b, off_b + 3 * wb)
    ub_ref[...] = zb[:, 0:wb] * jax.nn.sigmoid(zb[:, wb:2 * wb])
    sgb_ref[...] = _silu(zb[:, 2 * wb:3 * wb])

    off_c = off_b + 3 * wb
    sgc_ref[...] = _silu(proj(off_c + shift_w, w_ref.shape[1]))
    h_ext = jnp.concatenate([mod_norm(xp_ref[...]), h32, mod_norm(xn_ref[...])], axis=0)
    zc_ext = jnp.dot(h_ext.astype(BF16), w_ref[:, off_c:off_c + shift_w], preferred_element_type=F32)
    it = pl.program_id(0) % per_b
    prev_row = jnp.where(it == 0, 0.0, zc_ext[SUBLANES - 1:SUBLANES, :])
    next_row = jnp.where(it == per_b - 1, 0.0, zc_ext[SUBLANES + rows:SUBLANES + rows + 1, :])
    _rwkv_operands(zc_ext[SUBLANES:SUBLANES + rows, :], prev_row, next_row, mu_ref, w0_ref, w2_ref,
                   a0_ref, a2_ref, kk_ref, ka_ref, rk_ref, ones_ref, c_outs, kk_ref.shape[1])


def _inproj(x, ada, p, seq, seg):
    n, d = x.shape
    w_in_bf, conv_a = p['w_in'], p['conv_a']
    d_in = w_in_bf.shape[1]
    wa = conv_a.shape[1]
    wb = p['conv_b'].shape[1]
    shift_w = p['mu'].shape[0]
    wc = p['k_k'].shape[0]
    assert d_in == 4 * wa + 3 * wb + shift_w + wc
    assert seq % TOK_TILE == 0 and TOK_TILE % seg == 0
    per_b = seq // TOK_TILE
    nb = ada.shape[0]
    ada_map = (lambda i: (i // per_b, 0, 0)) if nb > 1 else (lambda i: (0, 0, 0))
    sub = TOK_TILE // SUBLANES
    n_sub = n // SUBLANES
    tok = lambda w: pl.BlockSpec((TOK_TILE, w), lambda i: (i, 0))
    full = lambda a: pl.BlockSpec(a.shape, lambda i: (0,) * a.ndim)
    params = (p['g_pre'].reshape(1, d), w_in_bf, conv_a, p['mu'].reshape(1, shift_w), p['w0'],
              p['w2'].reshape(2 * LORA, wc).astype(BF16), p['a0'],
              p['a2'].reshape(2 * LORA, wc).astype(BF16), p['k_k'].reshape(1, wc),
              p['k_a'].reshape(1, wc), p['r_k'].reshape(1, wc), p['ones_g'])
    outs = [(wa, BF16), (wb, F32), (wb, F32), (wc, F32)] + [
        (wc, BF16 if k == 1 else F32) for k in range(10)]
    return pl.pallas_call(
        functools.partial(_inproj_kernel, d=d, wa=wa, wb=wb, shift_w=shift_w, seg=seg, per_b=per_b),
        grid=(n // TOK_TILE,),
        in_specs=[tok(d),
                  pl.BlockSpec((SUBLANES, d), lambda i: (jnp.maximum(i * sub - 1, 0), 0)),
                  pl.BlockSpec((SUBLANES, d), lambda i: (jnp.minimum((i + 1) * sub, n_sub - 1), 0)),
                  pl.BlockSpec((1, 1, 3 * d), ada_map)] + [full(a) for a in params],
        out_specs=[tok(w) for w, _ in outs],
        out_shape=[jax.ShapeDtypeStruct((n, w), dt) for w, dt in outs],
        compiler_params=_cparams(("parallel",)),
        name="inproj",
    )(x, x, x, ada, *params)


def _convb_finish(u, sgb, lg_ref, lb_ref):
    mu = jnp.mean(u, -1, keepdims=True)
    var = jnp.mean(jnp.square(u - mu), -1, keepdims=True)
    ln = (u - mu) * lax.rsqrt(var + LN_EPS) * lg_ref[...] + lb_ref[...]
    return _silu(ln) * sgb


def _convb_grid_kernel(u_ref, sgb_ref, w_ref, bias_ref, lg_ref, lb_ref, o_ref, *, seq):
    n_rows = seq // GRID_W
    for r in range(n_rows):
        accs = [None, None]
        for k, q in enumerate(range(max(0, r - CONV_B_HALF), min(n_rows - 1, r + CONV_B_HALF) + 1)):
            j = q - r + CONV_B_HALF
            term = w_ref[j:j + 1, :] * u_ref[q * GRID_W:(q + 1) * GRID_W, :]
            accs[k % 2] = term if accs[k % 2] is None else accs[k % 2] + term
        u = accs[0] + accs[1] + bias_ref[...]
        out = slice(r * GRID_W, (r + 1) * GRID_W)
        o_ref[out, :] = _convb_finish(u, sgb_ref[out, :], lg_ref, lb_ref).astype(o_ref.dtype)


def _convb_seq_kernel(u_ref, sgb_ref, w_ref, bias_ref, lg_ref, lb_ref, o_ref, pad_ref,
                      *, seq, pad_rows, row_block):
    width = pad_ref.shape[1]
    zeros = jnp.zeros((pad_rows, width), F32)
    pad_ref[0:pad_rows, :] = zeros
    pad_ref[pad_rows + seq:pad_rows + seq + pad_rows, :] = zeros
    pad_ref[pad_rows:pad_rows + seq, :] = u_ref[...]
    first = pad_rows - CONV_B_HALF
    for i in range(seq // row_block):
        start = i * row_block
        accs = [None, None]
        for j in range(2 * CONV_B_HALF + 1):
            term = w_ref[j:j + 1, :] * pad_ref[start + first + j:start + first + j + row_block, :]
            accs[j % 2] = term if accs[j % 2] is None else accs[j % 2] + term
        u = accs[0] + accs[1] + bias_ref[...]
        out = slice(start, start + row_block)
        o_ref[out, :] = _convb_finish(u, sgb_ref[out, :], lg_ref, lb_ref).astype(o_ref.dtype)


def _convb(ub, sgb, conv_b, bias, ln_g, ln_b, seq, latent):
    n, w = ub.shape
    blk = pl.BlockSpec((seq, w), lambda b: (b, 0))
    vec = pl.BlockSpec((1, w), lambda b: (0, 0))
    if latent:
        body = functools.partial(_convb_grid_kernel, seq=seq)
        scratch = []
    else:
        pad_rows = -(-CONV_B_HALF // SUBLANES) * SUBLANES
        body = functools.partial(_convb_seq_kernel, seq=seq, pad_rows=pad_rows, row_block=64)
        scratch = [pltpu.VMEM((seq + 2 * pad_rows, w), F32)]
    return pl.pallas_call(
        body,
        grid=(n // seq,),
        in_specs=[blk, blk, pl.BlockSpec(conv_b.shape, lambda b: (0, 0)), vec, vec, vec],
        out_specs=blk,
        out_shape=jax.ShapeDtypeStruct((n, w), BF16),
        scratch_shapes=scratch,
        compiler_params=_cparams(("parallel",)),
        name="conv_b",
    )(ub, sgb, conv_b, bias.reshape(1, w), ln_g.reshape(1, w), ln_b.reshape(1, w))


def _bd(y, bdmask):
    return jnp.where(bdmask, jnp.concatenate([y] * HEADS_PER_GROUP, axis=0), 0.0).astype(BF16)


def _chunk_operands(reverse, r, v, kk, logw, kd, b, masks):
    tri3, strict, incl, bdmask = masks
    c = r.shape[0]
    cs = jnp.dot(tri3, jnp.concatenate(_split3(logw), axis=0), preferred_element_type=F32)
    yield
    p = jnp.exp(cs)
    p_prev = jnp.exp(cs - logw)
    p_inv = jnp.exp(-cs)
    p_tot = p[0:1, :] if reverse else p[c - 1:c, :]
    rt = r * p
    at = -(kk * p_prev)
    bt = b * p_inv
    kt = kd * p_inv

    ar = jnp.concatenate([at, rt], axis=0)
    sb = _dot_nt(ar, _bd(bt, bdmask))
    sk = _dot_nt(ar, _bd(kt, bdmask))
    yield
    l_ab = jnp.where(strict, sb[0:c], 0.0)
    m_rb = jnp.where(incl, sb[c:2 * c], 0.0)
    m_ak = jnp.where(strict, sk[0:c], 0.0)
    m_rk = jnp.where(incl, sk[c:2 * c], 0.0)

    wv = _dot(jnp.concatenate([m_ak, m_rk], axis=0), _bd(v, bdmask))

    pw = l_ab
    out = _dot(jnp.concatenate([pw, m_rb], axis=0), _bd(pw, bdmask))
    yield
    t_off = l_ab
    mt = m_rb + out[c:2 * c]
    pw = out[0:c]
    n_steps = int(math.log2(c))
    for step in range(1, n_steps):
        lhs = [t_off, mt] if step == n_steps - 1 else [t_off, mt, pw]
        out = _dot(jnp.concatenate(lhs, axis=0), _bd(pw, bdmask))
        yield
        t_off = t_off + pw + out[0:c]
        mt = mt + out[c:2 * c]
        pw = out[2 * c:3 * c] if step < n_steps - 1 else None

    tx = _dot(jnp.concatenate([t_off, mt], axis=0),
              jnp.concatenate([_bd(at, bdmask), _bd(wv[0:c], bdmask)], axis=1))
    yield
    a_hat, w2 = at + tx[0:c, 0:GROUP_W], wv[0:c] + tx[0:c, GROUP_W:2 * GROUP_W]
    r_hat = rt + tx[c:2 * c, 0:GROUP_W]
    y0 = tx[c:2 * c, GROUP_W:2 * GROUP_W] + wv[c:2 * c]
    ar_hat = jnp.concatenate([a_hat, r_hat], axis=0).astype(BF16)
    wy = jnp.concatenate([w2, y0], axis=0)
    bk_t = jnp.concatenate([bt * p_tot, kt * p_tot], axis=0).T.astype(BF16)
    p_col = jnp.broadcast_to(p_tot, (2 * c, GROUP_W)).T
    p_col = jnp.concatenate([p_col, p_col], axis=1)
    return ar_hat, wy, bk_t, p_col, v


def _interleave(tasks):
    results = [None] * len(tasks)
    live = list(range(len(tasks)))
    while live:
        for k in list(live):
            try:
                next(tasks[k])
            except StopIteration as done:
                results[k] = done.value
                live.remove(k)
    return results


def _state_chain(st, operands, y_ref, row_slices, lanes, bdmask):
    for (ar_hat, wy, bk_t, p_col, v), rows in zip(operands, row_slices):
        c = v.shape[0]
        uy = _dot(ar_hat, st) + wy
        yield
        u, y = uy[0:c], uy[c:2 * c]
        y_ref[rows, lanes] = y
        uv = jnp.concatenate([u, v.astype(F32)], axis=0)
        st = jnp.where(bdmask, st * p_col + _dot(bk_t, uv), 0.0)
        yield
    return st


def _scan_kernel(*refs, has_s0):
    if has_s0:
        s0_ref, refs = refs[0], refs[1:]
    (rf, vf, kf, lwf, kdf, bf, rb, vb, kb, lwb, kdb, bb, yf_o, yb_o, st_o) = refs
    i = pl.program_id(1)
    c = CHUNK
    n_chunks = rf.shape[0] // c
    n_groups = rf.shape[1] // GROUP_W

    row = lax.broadcasted_iota(jnp.int32, (GROUP_W, GROUP_W), 0)
    colm = lax.broadcasted_iota(jnp.int32, (GROUP_W, GROUP_W), 1)
    bdmask = (row // HEAD) == (colm // HEAD)
    t_row = lax.broadcasted_iota(jnp.int32, (c, GROUP_W), 0)
    t_col = lax.broadcasted_iota(jnp.int32, (c, GROUP_W), 1) % c
    tr = lax.broadcasted_iota(jnp.int32, (c, 3 * c), 0)
    tc = lax.broadcasted_iota(jnp.int32, (c, 3 * c), 1) % c
    masks_f = ((tc <= tr).astype(BF16), t_col < t_row, t_col <= t_row, bdmask)
    masks_b = ((tc >= tr).astype(BF16), t_col > t_row, t_col >= t_row, bdmask)

    @pl.when(i == 0)
    def _():
        if has_s0:
            for g in range(n_groups):
                for d in range(2):
                    st_o[0, g, d] = jnp.where(bdmask, s0_ref[0, g, d], 0.0)
        else:
            st_o[...] = jnp.zeros(st_o.shape, F32)

    rows = [pl.ds(j * c, c) for j in range(n_chunks)]
    lanes = [pl.ds(g * GROUP_W, GROUP_W) for g in range(n_groups)]
    fwd_refs = (rf, vf, kf, lwf, kdf, bf)
    bwd_refs = (rb, vb, kb, lwb, kdb, bb)
    ops = _interleave(
        [_chunk_operands(False, *(ref[s, l] for ref in fwd_refs), masks_f) for l in lanes for s in rows]
        + [_chunk_operands(True, *(ref[s, l] for ref in bwd_refs), masks_b) for l in lanes for s in rows])
    chains = []
    for g in range(n_groups):
        of = ops[g * n_chunks:(g + 1) * n_chunks]
        ob = ops[(n_groups + g) * n_chunks:(n_groups + g + 1) * n_chunks]
        chains.append(_state_chain(st_o[0, g, 0], of, yf_o, rows, lanes[g], bdmask))
        chains.append(_state_chain(st_o[0, g, 1], ob[::-1], yb_o, rows[::-1], lanes[g], bdmask))
    finals = _interleave(chains)
    for g in range(n_groups):
        st_o[0, g, 0] = finals[2 * g]
        st_o[0, g, 1] = finals[2 * g + 1]


def _scan(prep, s0, seq, emit_state):
    r, v, kn, lw0, kd0, b0, lw1, kd1, b1 = prep
    n, wc = r.shape
    nb = n // seq
    ng = wc // GROUP_W
    nt = seq // SCAN_BLOCK
    fwd = pl.BlockSpec((SCAN_BLOCK, wc), lambda b, i: (b * nt + i, 0))
    bwd = pl.BlockSpec((SCAN_BLOCK, wc), lambda b, i: (b * nt + nt - 1 - i, 0))
    st_shape = (1, ng, 2, GROUP_W, GROUP_W)
    st_spec = pl.BlockSpec(st_shape, lambda b, i: (b, 0, 0, 0, 0))
    ins = [r, v, kn, lw0, kd0, b0, r, v, kn, lw1, kd1, b1]
    in_specs = [fwd] * 6 + [bwd] * 6
    if s0 is not None:
        ins = [s0] + ins
        in_specs = [st_spec] + in_specs
    y_shape = jax.ShapeDtypeStruct((n, wc), F32)
    out_specs, out_shape, scratch = [fwd, bwd], [y_shape, y_shape], []
    if emit_state:
        out_specs.append(st_spec)
        out_shape.append(jax.ShapeDtypeStruct((nb,) + st_shape[1:], F32))
    else:
        scratch.append(pltpu.VMEM(st_shape, F32))
    outs = pl.pallas_call(
        functools.partial(_scan_kernel, has_s0=s0 is not None),
        grid=(nb, nt),
        in_specs=in_specs,
        out_specs=out_specs,
        out_shape=out_shape,
        scratch_shapes=scratch,
        compiler_params=_cparams(("parallel", "arbitrary")),
        name="rwkv_scan",
    )(*ins)
    return (outs[0], outs[1], outs[2] if emit_state else None)


def _outproj_kernel(x_ref, ada_ref, ya_ref, yb_ref, yf_ref, ybw_ref, bonus_ref, sgc_ref,
                    gng_ref, gnb_ref, ones_ref, w_ref, gp_ref, o_ref, *, d):
    ones_g = ones_ref[...]
    ys = yf_ref[...] + ybw_ref[...]
    mu = _head_sum(ys, ones_g) * (1.0 / HEAD)
    dev = ys - mu
    var = _head_sum(dev * dev, ones_g) * (1.0 / HEAD)
    yg = dev * lax.rsqrt(var + GN_EPS) * gng_ref[...] + gnb_ref[...]
    yc = (yg + bonus_ref[...]) * sgc_ref[...]
    cat = jnp.concatenate([ya_ref[...], yb_ref[...], yc.astype(BF16)], axis=-1)
    out = jnp.dot(cat, w_ref[...], preferred_element_type=F32)
    gate = ada_ref[0][:, 2 * d:3 * d]
    nrm = out * lax.rsqrt(jnp.mean(out * out, -1, keepdims=True) + RMS_EPS) * gp_ref[...]
    o_ref[...] = x_ref[...] + gate * nrm


def _outproj(x, ada, ya, yb, yf, ybw, bonus, sgc, gn_g, gn_b, ones_g, w_out_bf, g_post, seq):
    n, d = x.shape
    wc = yf.shape[1]
    wa = ya.shape[1]
    per_b = seq // TOK_TILE
    nb = ada.shape[0]
    ada_map = (lambda i: (i // per_b, 0, 0)) if nb > 1 else (lambda i: (0, 0, 0))
    tok = lambda w: pl.BlockSpec((TOK_TILE, w), lambda i: (i, 0))
    full = lambda a: pl.BlockSpec(a.shape, lambda i: (0,) * a.ndim)
    small = (gn_g.reshape(1, wc), gn_b.reshape(1, wc), ones_g, w_out_bf, g_post.reshape(1, d))
    return pl.pallas_call(
        functools.partial(_outproj_kernel, d=d),
        grid=(n // TOK_TILE,),
        in_specs=[tok(d), pl.BlockSpec((1, 1, 3 * d), ada_map), tok(wa), tok(wa), tok(wc), tok(wc),
                  tok(wc), tok(wc)] + [full(a) for a in small],
        out_specs=tok(d),
        out_shape=jax.ShapeDtypeStruct((n, d), F32),
        compiler_params=_cparams(("parallel",)),
        name="outproj",
    )(x, ada, ya, yb, yf, ybw, bonus, sgc, *small)


def _trunk_layer(x, ada, p, s0, seq, latent):
    ya, ub, sgb, sgc, *ops = _inproj(x, ada, p, seq, GRID_W if latent else seq)
    yb = _convb(ub, sgb, p['conv_b'], p['conv_b_bias'], p['ln_b_g'], p['ln_b_b'], seq, latent)
    yf, ybw, st = _scan(ops[:9], s0, seq, emit_state=not latent)
    x = _outproj(x, ada, ya, yb, yf, ybw, ops[9], sgc, p['gn_g'], p['gn_b'], p['ones_g'],
                 p['w_out'], p['g_post'], seq)
    return x, st


def _pack_states(s):
    b, nd, h, hv, hk = s.shape
    g = h // HEADS_PER_GROUP
    st = jnp.swapaxes(s, -1, -2).reshape(b, nd, g, HEADS_PER_GROUP * hk, hv)
    st = jnp.tile(st, (1, 1, 1, 1, HEADS_PER_GROUP))
    return jnp.swapaxes(st, 1, 2)


def _unpack_states(st):
    b, g, nd = st.shape[:3]
    diag = jnp.stack([st[:, :, :, h * HEAD:(h + 1) * HEAD, h * HEAD:(h + 1) * HEAD]
                      for h in range(HEADS_PER_GROUP)], axis=3)
    diag = jnp.swapaxes(diag, -1, -2)
    return jnp.swapaxes(diag, 1, 2).reshape(b, nd, g * HEADS_PER_GROUP, HEAD, HEAD)


def kernel(x_prompt, x_sample, c, state_rwkv, c_ctx, ada_w, ada_b, g_pre, g_post, w_in, conv_a,
           conv_b, conv_b_bias, ln_b_g, ln_b_b, mu, w0, w2, a0, a2, k_k, k_a, r_k, gn_g, gn_b, w_out):
    bp, seq_p, d = x_prompt.shape
    bs, seq_s, _ = x_sample.shape
    depth = ada_w.shape[0]

    rows = -(-(1 + bs) // SUBLANES) * SUBLANES
    mods = jnp.zeros((rows, d), F32).at[0].set(c_ctx).at[1:1 + bs].set(c)
    ada = _ada(mods, ada_w, ada_b)
    head_id = jnp.arange(GROUP_W) // HEAD
    ones_g = (head_id[:, None] == head_id[None, :]).astype(BF16)
    w_in_bf = w_in.astype(BF16)
    w_out_bf = w_out.astype(BF16)

    xp = x_prompt.reshape(bp * seq_p, d)
    xs = x_sample.reshape(bs * seq_s, d)
    new_states = []
    for l in range(depth):
        p = {'g_pre': g_pre[l], 'g_post': g_post[l], 'w_in': w_in_bf[l], 'conv_a': conv_a[l],
             'conv_b': conv_b[l], 'conv_b_bias': conv_b_bias[l], 'ln_b_g': ln_b_g[l],
             'ln_b_b': ln_b_b[l], 'mu': mu[l], 'w0': w0[l], 'w2': w2[l], 'a0': a0[l], 'a2': a2[l],
             'k_k': k_k[l], 'k_a': k_a[l], 'r_k': r_k[l].reshape(-1), 'gn_g': gn_g[l],
             'gn_b': gn_b[l], 'w_out': w_out_bf[l], 'ones_g': ones_g}
        xp, st = _trunk_layer(xp, ada[l, 0:1].reshape(1, 1, 3 * d), p, None, seq_p, False)
        new_states.append(_unpack_states(st))
        xs, _ = _trunk_layer(xs, ada[l, 1:1 + bs].reshape(bs, 1, 3 * d), p,
                             _pack_states(state_rwkv[:, l]), seq_s, True)
    return (xp.reshape(bp, seq_p, d), xs.reshape(bs, seq_s, d), jnp.stack(new_states, axis=1))
```

```python
import functools
import math

import jax
import jax.numpy as jnp
from jax import lax
from jax.experimental import pallas as pl
from jax.experimental.pallas import tpu as pltpu

F32 = jnp.float32
BF16 = jnp.bfloat16

SUBLANES = 8
GRID_W = 64
HEAD = 64
HEADS_PER_GROUP = 4
GROUP_W = HEAD * HEADS_PER_GROUP
CHUNK = 64
CONV_B_HALF = 15
LORA = 64
RMS_EPS = 1e-6
LN_EPS = 1e-5
GN_EPS = 64e-5
DECAY_SCALE = math.exp(-0.5)

TOK_TILE = 512
SCAN_BLOCK = 512
SCAN_SKEW = 3
VMEM_LIMIT = 56 * 1024 * 1024


def _cparams(sem):
    return pltpu.CompilerParams(dimension_semantics=sem, vmem_limit_bytes=VMEM_LIMIT)


def _dot(a, b):
    return jnp.dot(a.astype(BF16), b.astype(BF16), preferred_element_type=F32)


def _dot_nt(a, b):
    return lax.dot_general(a.astype(BF16), b.astype(BF16), (((1,), (1,)), ((), ())),
                           preferred_element_type=F32)


def _split3(x):
    h1 = x.astype(BF16)
    r1 = x - h1.astype(F32)
    h2 = r1.astype(BF16)
    h3 = (r1 - h2.astype(F32)).astype(BF16)
    return h1, h2, h3


def _head_sum(x, ones_g):
    parts = [_dot(x[:, g * GROUP_W:(g + 1) * GROUP_W], ones_g) for g in range(x.shape[1] // GROUP_W)]
    return jnp.concatenate(parts, axis=1)


def _silu(x):
    return x * jax.nn.sigmoid(x)


def _ada_kernel(mod_ref, w_ref, b_ref, o_ref):
    m = mod_ref[...]
    o_ref[0] = jnp.dot(_silu(m), w_ref[0], preferred_element_type=F32,
                       precision=lax.Precision.HIGHEST) + b_ref[0]


def _ada(mods, ada_w, ada_b):
    n_layers, d, d3 = ada_w.shape
    rows = mods.shape[0]
    blk = d
    return pl.pallas_call(
        _ada_kernel,
        grid=(n_layers, d3 // blk),
        in_specs=[pl.BlockSpec((rows, d), lambda l, j: (0, 0)),
                  pl.BlockSpec((1, d, blk), lambda l, j: (l, 0, j)),
                  pl.BlockSpec((1, 1, blk), lambda l, j: (l, 0, j))],
        out_specs=pl.BlockSpec((1, rows, blk), lambda l, j: (l, 0, j)),
        out_shape=jax.ShapeDtypeStruct((n_layers, rows, d3), F32),
        compiler_params=_cparams(("parallel", "parallel")),
        name="ada",
    )(mods, ada_w, ada_b.reshape(n_layers, 1, d3))


def _rwkv_operands(zc, prev_row, next_row, mu_ref, w0_ref, w2_ref, a0_ref, a2_ref, kk_ref, ka_ref,
                   rk_ref, ones_ref, outs, wc):
    r_o, v_o, kn_o, lw0_o, kd0_o, b0_o, lw1_o, kd1_o, b1_o, bonus_o = outs
    rows = zc.shape[0]
    ridx = lax.broadcasted_iota(jnp.int32, zc.shape, 0)
    up = jnp.where(ridx == 0, prev_row, pltpu.roll(zc, 1, 0))
    dn = jnp.where(ridx == rows - 1, next_row, pltpu.roll(zc, rows - 1, 0))
    zs = zc + mu_ref[...] * (0.5 * (up + dn) - zc)

    r, k, v = zs[:, 0:wc], zs[:, wc:2 * wc], zs[:, 2 * wc:3 * wc]
    lw = jnp.tanh(zs[:, 3 * wc:3 * wc + 2 * LORA])
    la = zs[:, 3 * wc + 2 * LORA:3 * wc + 4 * LORA]
    lane = lax.broadcasted_iota(jnp.int32, lw.shape, 1)
    ones_g = ones_ref[...]

    r_o[...] = r
    v_o[...] = v.astype(v_o.dtype)
    kk = k * kk_ref[...]
    kk_ss = _head_sum(kk * kk, ones_g)
    yield
    kn = kk * lax.rsqrt(kk_ss + 1e-12)
    kn_o[...] = kn

    kd_sum = jnp.zeros_like(k)
    for d, (lw_o, kd_o, b_o) in enumerate(((lw0_o, kd0_o, b0_o), (lw1_o, kd1_o, b1_o))):
        sel = (lane >= d * LORA) & (lane < (d + 1) * LORA)
        wl = w0_ref[d:d + 1, :] + _dot(jnp.where(sel, lw, 0.0), w2_ref[...])
        a = jax.nn.sigmoid(a0_ref[d:d + 1, :] + _dot(jnp.where(sel, la, 0.0), a2_ref[...]))
        yield
        kd = k * (1.0 + (a - 1.0) * ka_ref[...])
        lw_o[...] = -DECAY_SCALE * jax.nn.sigmoid(wl)
        kd_o[...] = kd
        b_o[...] = kn * a
        kd_sum = kd_sum + kd
    bonus_o[...] = _head_sum(r * kd_sum * rk_ref[...], ones_g) * v


def _inproj_kernel(x_ref, xp_ref, xn_ref, ada_ref, g_ref, w_ref, ca_ref, mu_ref, w0_ref, w2_ref,
                   a0_ref, a2_ref, kk_ref, ka_ref, rk_ref, ones_ref,
                   ya_ref, ub_ref, sgb_ref, sgc_ref, *c_outs, d, wa, wb, shift_w, seg, per_b):
    ada = ada_ref[0]
    shift, scale = ada[:, 0:d], ada[:, d:2 * d]

    def mod_norm(x):
        y = x * lax.rsqrt(jnp.mean(x * x, -1, keepdims=True) + RMS_EPS) * g_ref[...]
        return y * (1.0 + scale) + shift

    h32 = mod_norm(x_ref[...])
    h = h32.astype(BF16)
    rows = h.shape[0]
    proj = lambda lo, hi: jnp.dot(h, w_ref[:, lo:hi], preferred_element_type=F32)
    off_b = 4 * wa
    off_c = off_b + 3 * wb

    h_ext = jnp.concatenate([mod_norm(xp_ref[...]), h32, mod_norm(xn_ref[...])], axis=0)
    zc_ext = jnp.dot(h_ext.astype(BF16), w_ref[:, off_c:off_c + shift_w], preferred_element_type=F32)
    it = pl.program_id(0) % per_b
    prev_row = jnp.where(it == 0, 0.0, zc_ext[SUBLANES - 1:SUBLANES, :])
    next_row = jnp.where(it == per_b - 1, 0.0, zc_ext[SUBLANES + rows:SUBLANES + rows + 1, :])

    def branches():
        za = proj(0, off_b)
        yield
        xa, bg, cg, ga = (za[:, k * wa:(k + 1) * wa] for k in range(4))
        u = cg * xa
        pos = lax.broadcasted_iota(jnp.int32, u.shape, 0) % seg
        prev = jnp.where(pos == 0, 0.0, pltpu.roll(u, 1, 0))
        nxt = jnp.where(pos == seg - 1, 0.0, pltpu.roll(u, rows - 1, 0))
        conv = ca_ref[0:1, :] * prev + ca_ref[1:2, :] * u + ca_ref[2:3, :] * nxt
        ya_ref[...] = (bg * conv * _silu(ga)).astype(ya_ref.dtype)
        zb = proj(off_b, off_c)
        yield
        ub_ref[...] = zb[:, 0:wb] * jax.nn.sigmoid(zb[:, wb:2 * wb])
        sgb_ref[...] = _silu(zb[:, 2 * wb:3 * wb])
        sgc_ref[...] = _silu(proj(off_c + shift_w, w_ref.shape[1]))

    tasks = [branches(),
             _rwkv_operands(zc_ext[SUBLANES:SUBLANES + rows, :], prev_row, next_row, mu_ref, w0_ref,
                            w2_ref, a0_ref, a2_ref, kk_ref, ka_ref, rk_ref, ones_ref, c_outs,
                            kk_ref.shape[1])]
    _interleave(tasks, [0, 0], [None, None])


def _inproj(x, ada, p, seq, seg):
    n, d = x.shape
    w_in_bf, conv_a = p['w_in'], p['conv_a']
    d_in = w_in_bf.shape[1]
    wa = conv_a.shape[1]
    wb = p['conv_b'].shape[1]
    shift_w = p['mu'].shape[0]
    wc = p['k_k'].shape[0]
    assert d_in == 4 * wa + 3 * wb + shift_w + wc
    tile = min(TOK_TILE, seq)
    assert seq % tile == 0 and tile % seg == 0
    per_b = seq // tile
    nb = ada.shape[0]
    ada_map = (lambda i: (i // per_b, 0, 0)) if nb > 1 else (lambda i: (0, 0, 0))
    sub = tile // SUBLANES
    n_sub = n // SUBLANES
    tok = lambda w: pl.BlockSpec((tile, w), lambda i: (i, 0))
    full = lambda a: pl.BlockSpec(a.shape, lambda i: (0,) * a.ndim)
    params = (p['g_pre'].reshape(1, d), w_in_bf, conv_a, p['mu'].reshape(1, shift_w), p['w0'],
              p['w2'].reshape(2 * LORA, wc).astype(BF16), p['a0'],
              p['a2'].reshape(2 * LORA, wc).astype(BF16), p['k_k'].reshape(1, wc),
              p['k_a'].reshape(1, wc), p['r_k'].reshape(1, wc), p['ones_g'])
    outs = [(wa, BF16), (wb, F32), (wb, F32), (wc, F32)] + [
        (wc, BF16 if k == 1 else F32) for k in range(10)]
    return pl.pallas_call(
        functools.partial(_inproj_kernel, d=d, wa=wa, wb=wb, shift_w=shift_w, seg=seg, per_b=per_b),
        grid=(n // tile,),
        in_specs=[tok(d),
                  pl.BlockSpec((SUBLANES, d), lambda i: (jnp.maximum(i * sub - 1, 0), 0)),
                  pl.BlockSpec((SUBLANES, d), lambda i: (jnp.minimum((i + 1) * sub, n_sub - 1), 0)),
                  pl.BlockSpec((1, 1, 3 * d), ada_map)] + [full(a) for a in params],
        out_specs=[tok(w) for w, _ in outs],
        out_shape=[jax.ShapeDtypeStruct((n, w), dt) for w, dt in outs],
        compiler_params=_cparams(("parallel",)),
        name="inproj",
    )(x, x, x, ada, *params)


def _convb_finish(u, sgb, lg_ref, lb_ref):
    mu = jnp.mean(u, -1, keepdims=True)
    var = jnp.mean(jnp.square(u - mu), -1, keepdims=True)
    ln = (u - mu) * lax.rsqrt(var + LN_EPS) * lg_ref[...] + lb_ref[...]
    return _silu(ln) * sgb


def _convb_grid_kernel(u_ref, sgb_ref, w_ref, bias_ref, lg_ref, lb_ref, o_ref, *, seq):
    n_rows = seq // GRID_W
    for r in range(n_rows):
        accs = [None, None]
        for k, q in enumerate(range(max(0, r - CONV_B_HALF), min(n_rows - 1, r + CONV_B_HALF) + 1)):
            j = q - r + CONV_B_HALF
            term = w_ref[j:j + 1, :] * u_ref[q * GRID_W:(q + 1) * GRID_W, :]
            accs[k % 2] = term if accs[k % 2] is None else accs[k % 2] + term
        u = accs[0] + accs[1] + bias_ref[...]
        out = slice(r * GRID_W, (r + 1) * GRID_W)
        o_ref[out, :] = _convb_finish(u, sgb_ref[out, :], lg_ref, lb_ref).astype(o_ref.dtype)


def _convb_seq_kernel(u_ref, sgb_ref, w_ref, bias_ref, lg_ref, lb_ref, o_ref, pad_ref,
                      *, seq, pad_rows, row_block):
    width = pad_ref.shape[1]
    zeros = jnp.zeros((pad_rows, width), F32)
    pad_ref[0:pad_rows, :] = zeros
    pad_ref[pad_rows + seq:pad_rows + seq + pad_rows, :] = zeros
    pad_ref[pad_rows:pad_rows + seq, :] = u_ref[...]
    first = pad_rows - CONV_B_HALF
    for i in range(seq // row_block):
        start = i * row_block
        accs = [None, None]
        for j in range(2 * CONV_B_HALF + 1):
            term = w_ref[j:j + 1, :] * pad_ref[start + first + j:start + first + j + row_block, :]
            accs[j % 2] = term if accs[j % 2] is None else accs[j % 2] + term
        u = accs[0] + accs[1] + bias_ref[...]
        out = slice(start, start + row_block)
        o_ref[out, :] = _convb_finish(u, sgb_ref[out, :], lg_ref, lb_ref).astype(o_ref.dtype)


def _convb(ub, sgb, conv_b, bias, ln_g, ln_b, seq, latent):
    n, w = ub.shape
    blk = pl.BlockSpec((seq, w), lambda b: (b, 0))
    vec = pl.BlockSpec((1, w), lambda b: (0, 0))
    if latent:
        body = functools.partial(_convb_grid_kernel, seq=seq)
        scratch = []
    else:
        pad_rows = -(-CONV_B_HALF // SUBLANES) * SUBLANES
        body = functools.partial(_convb_seq_kernel, seq=seq, pad_rows=pad_rows, row_block=64)
        scratch = [pltpu.VMEM((seq + 2 * pad_rows, w), F32)]
    return pl.pallas_call(
        body,
        grid=(n // seq,),
        in_specs=[blk, blk, pl.BlockSpec(conv_b.shape, lambda b: (0, 0)), vec, vec, vec],
        out_specs=blk,
        out_shape=jax.ShapeDtypeStruct((n, w), BF16),
        scratch_shapes=scratch,
        compiler_params=_cparams(("parallel",)),
        name="conv_b",
    )(ub, sgb, conv_b, bias.reshape(1, w), ln_g.reshape(1, w), ln_b.reshape(1, w))


def _bd(y, bdmask):
    return jnp.where(bdmask, jnp.concatenate([y] * HEADS_PER_GROUP, axis=0), 0.0).astype(BF16)


def _chunk_operands(reverse, r, v, kk, logw, kd, b, masks):
    tri3, strict, incl, bdmask = masks
    c = r.shape[0]
    cs = jnp.dot(tri3, jnp.concatenate(_split3(logw), axis=0), preferred_element_type=F32)
    yield
    p = jnp.exp(cs)
    p_prev = jnp.exp(cs - logw)
    p_inv = jnp.exp(-cs)
    p_tot = p[0:1, :] if reverse else p[c - 1:c, :]
    rt = r * p
    at = -(kk * p_prev)
    bt = b * p_inv
    kt = kd * p_inv

    ar = jnp.concatenate([at, rt], axis=0)
    sb = _dot_nt(ar, _bd(bt, bdmask))
    sk = _dot_nt(ar, _bd(kt, bdmask))
    yield
    l_ab = jnp.where(strict, sb[0:c], 0.0)
    m_rb = jnp.where(incl, sb[c:2 * c], 0.0)
    m_ak = jnp.where(strict, sk[0:c], 0.0)
    m_rk = jnp.where(incl, sk[c:2 * c], 0.0)

    wv = _dot(jnp.concatenate([m_ak, m_rk], axis=0), _bd(v, bdmask))

    pw = l_ab
    out = _dot(jnp.concatenate([pw, m_rb], axis=0), _bd(pw, bdmask))
    yield
    t_off = l_ab
    mt = m_rb + out[c:2 * c]
    pw = out[0:c]
    n_steps = int(math.log2(c))
    for step in range(1, n_steps):
        lhs = [t_off, mt] if step == n_steps - 1 else [t_off, mt, pw]
        out = _dot(jnp.concatenate(lhs, axis=0), _bd(pw, bdmask))
        yield
        t_off = t_off + pw + out[0:c]
        mt = mt + out[c:2 * c]
        pw = out[2 * c:3 * c] if step < n_steps - 1 else None

    tx = _dot(jnp.concatenate([t_off, mt], axis=0),
              jnp.concatenate([_bd(at, bdmask), _bd(wv[0:c], bdmask)], axis=1))
    yield
    a_hat, w2 = at + tx[0:c, 0:GROUP_W], wv[0:c] + tx[0:c, GROUP_W:2 * GROUP_W]
    r_hat = rt + tx[c:2 * c, 0:GROUP_W]
    y0 = tx[c:2 * c, GROUP_W:2 * GROUP_W] + wv[c:2 * c]
    ar_hat = jnp.concatenate([a_hat, r_hat], axis=0).astype(BF16)
    wy = jnp.concatenate([w2, y0], axis=0)
    bk_t = jnp.concatenate([bt * p_tot, kt * p_tot], axis=0).T.astype(BF16)
    p_col = jnp.broadcast_to(p_tot, (2 * c, GROUP_W)).T
    p_col = jnp.concatenate([p_col, p_col], axis=1)
    return ar_hat, wy, bk_t, p_col, v


def _interleave(tasks, starts, results):
    live = list(range(len(tasks)))
    rnd = 0
    while live:
        for k in list(live):
            if rnd >= starts[k]:
                try:
                    next(tasks[k])
                except StopIteration as done:
                    results[k] = done.value
                    live.remove(k)
        rnd += 1


def _state_chain(st, fetches, y_ref, row_slices, lanes, bdmask):
    for fetch, rows in zip(fetches, row_slices):
        while fetch() is None:
            yield
        ar_hat, wy, bk_t, p_col, v = fetch()
        c = v.shape[0]
        uy = _dot(ar_hat, st) + wy
        yield
        u, y = uy[0:c], uy[c:2 * c]
        y_ref[rows, lanes] = y
        uv = jnp.concatenate([u, v.astype(F32)], axis=0)
        st = jnp.where(bdmask, st * p_col + _dot(bk_t, uv), 0.0)
        yield
    return st


def _scan_kernel(*refs, has_s0):
    if has_s0:
        s0_ref, refs = refs[0], refs[1:]
    (rf, vf, kf, lwf, kdf, bf, rb, vb, kb, lwb, kdb, bb, yf_o, yb_o, st_o) = refs
    i = pl.program_id(1)
    c = CHUNK
    n_chunks = rf.shape[0] // c
    n_groups = rf.shape[1] // GROUP_W

    row = lax.broadcasted_iota(jnp.int32, (GROUP_W, GROUP_W), 0)
    colm = lax.broadcasted_iota(jnp.int32, (GROUP_W, GROUP_W), 1)
    bdmask = (row // HEAD) == (colm // HEAD)
    t_row = lax.broadcasted_iota(jnp.int32, (c, GROUP_W), 0)
    t_col = lax.broadcasted_iota(jnp.int32, (c, GROUP_W), 1) % c
    tr = lax.broadcasted_iota(jnp.int32, (c, 3 * c), 0)
    tc = lax.broadcasted_iota(jnp.int32, (c, 3 * c), 1) % c
    masks_f = ((tc <= tr).astype(BF16), t_col < t_row, t_col <= t_row, bdmask)
    masks_b = ((tc >= tr).astype(BF16), t_col > t_row, t_col >= t_row, bdmask)

    @pl.when(i == 0)
    def _():
        if has_s0:
            for g in range(n_groups):
                for d in range(2):
                    st_o[0, g, d] = jnp.where(bdmask, s0_ref[0, g, d], 0.0)
        else:
            st_o[...] = jnp.zeros(st_o.shape, F32)

    rows = [pl.ds(j * c, c) for j in range(n_chunks)]
    lanes = [pl.ds(g * GROUP_W, GROUP_W) for g in range(n_groups)]
    fwd_refs = (rf, vf, kf, lwf, kdf, bf)
    bwd_refs = (rb, vb, kb, lwb, kdb, bb)
    tasks, starts = [], []
    for l in lanes:
        for j, s in enumerate(rows):
            tasks.append(_chunk_operands(False, *(ref[s, l] for ref in fwd_refs), masks_f))
            starts.append(j * SCAN_SKEW)
    for l in lanes:
        for j, s in enumerate(rows):
            tasks.append(_chunk_operands(True, *(ref[s, l] for ref in bwd_refs), masks_b))
            starts.append((n_chunks - 1 - j) * SCAN_SKEW)
    n_ops = len(tasks)
    results = [None] * (n_ops + 2 * n_groups)
    fetch = lambda k: (lambda: results[k])
    for g in range(n_groups):
        f_ids = [g * n_chunks + j for j in range(n_chunks)]
        b_ids = [(n_groups + g) * n_chunks + j for j in range(n_chunks)]
        tasks.append(_state_chain(st_o[0, g, 0], [fetch(k) for k in f_ids], yf_o, rows, lanes[g],
                                  bdmask))
        tasks.append(_state_chain(st_o[0, g, 1], [fetch(k) for k in b_ids[::-1]], yb_o, rows[::-1],
                                  lanes[g], bdmask))
        starts += [0, 0]
    _interleave(tasks, starts, results)
    for g in range(n_groups):
        st_o[0, g, 0] = results[n_ops + 2 * g]
        st_o[0, g, 1] = results[n_ops + 2 * g + 1]


def _scan(prep, s0, seq, emit_state):
    r, v, kn, lw0, kd0, b0, lw1, kd1, b1 = prep
    n, wc = r.shape
    nb = n // seq
    ng = wc // GROUP_W
    blk = min(SCAN_BLOCK, seq)
    nt = seq // blk
    fwd = pl.BlockSpec((blk, wc), lambda b, i: (b * nt + i, 0))
    bwd = pl.BlockSpec((blk, wc), lambda b, i: (b * nt + nt - 1 - i, 0))
    st_shape = (1, ng, 2, GROUP_W, GROUP_W)
    st_spec = pl.BlockSpec(st_shape, lambda b, i: (b, 0, 0, 0, 0))
    ins = [r, v, kn, lw0, kd0, b0, r, v, kn, lw1, kd1, b1]
    in_specs = [fwd] * 6 + [bwd] * 6
    if s0 is not None:
        ins = [s0] + ins
        in_specs = [st_spec] + in_specs
    y_shape = jax.ShapeDtypeStruct((n, wc), F32)
    out_specs, out_shape, scratch = [fwd, bwd], [y_shape, y_shape], []
    if emit_state:
        out_specs.append(st_spec)
        out_shape.append(jax.ShapeDtypeStruct((nb,) + st_shape[1:], F32))
    else:
        scratch.append(pltpu.VMEM(st_shape, F32))
    outs = pl.pallas_call(
        functools.partial(_scan_kernel, has_s0=s0 is not None),
        grid=(nb, nt),
        in_specs=in_specs,
        out_specs=out_specs,
        out_shape=out_shape,
        scratch_shapes=scratch,
        compiler_params=_cparams(("parallel", "arbitrary")),
        name="rwkv_scan",
    )(*ins)
    return (outs[0], outs[1], outs[2] if emit_state else None)


def _outproj_kernel(x_ref, ada_ref, ya_ref, yb_ref, yf_ref, ybw_ref, bonus_ref, sgc_ref,
                    gng_ref, gnb_ref, ones_ref, w_ref, gp_ref, o_ref, *, d):
    ones_g = ones_ref[...]
    ys = yf_ref[...] + ybw_ref[...]
    mu = _head_sum(ys, ones_g) * (1.0 / HEAD)
    dev = ys - mu
    var = _head_sum(dev * dev, ones_g) * (1.0 / HEAD)
    yg = dev * lax.rsqrt(var + GN_EPS) * gng_ref[...] + gnb_ref[...]
    yc = (yg + bonus_ref[...]) * sgc_ref[...]
    cat = jnp.concatenate([ya_ref[...], yb_ref[...], yc.astype(BF16)], axis=-1)
    out = jnp.dot(cat, w_ref[...], preferred_element_type=F32)
    gate = ada_ref[0][:, 2 * d:3 * d]
    nrm = out * lax.rsqrt(jnp.mean(out * out, -1, keepdims=True) + RMS_EPS) * gp_ref[...]
    o_ref[...] = x_ref[...] + gate * nrm


def _outproj(x, ada, ya, yb, yf, ybw, bonus, sgc, gn_g, gn_b, ones_g, w_out_bf, g_post, seq):
    n, d = x.shape
    wc = yf.shape[1]
    wa = ya.shape[1]
    tile = min(TOK_TILE, seq)
    per_b = seq // tile
    nb = ada.shape[0]
    ada_map = (lambda i: (i // per_b, 0, 0)) if nb > 1 else (lambda i: (0, 0, 0))
    tok = lambda w: pl.BlockSpec((tile, w), lambda i: (i, 0))
    full = lambda a: pl.BlockSpec(a.shape, lambda i: (0,) * a.ndim)
    small = (gn_g.reshape(1, wc), gn_b.reshape(1, wc), ones_g, w_out_bf, g_post.reshape(1, d))
    return pl.pallas_call(
        functools.partial(_outproj_kernel, d=d),
        grid=(n // tile,),
        in_specs=[tok(d), pl.BlockSpec((1, 1, 3 * d), ada_map), tok(wa), tok(wa), tok(wc), tok(wc),
                  tok(wc), tok(wc)] + [full(a) for a in small],
        out_specs=tok(d),
        out_shape=jax.ShapeDtypeStruct((n, d), F32),
        compiler_params=_cparams(("parallel",)),
        name="outproj",
    )(x, ada, ya, yb, yf, ybw, bonus, sgc, *small)


def _trunk_layer(x, ada, p, s0, seq, latent):
    ya, ub, sgb, sgc, *ops = _inproj(x, ada, p, seq, GRID_W if latent else seq)
    yb = _convb(ub, sgb, p['conv_b'], p['conv_b_bias'], p['ln_b_g'], p['ln_b_b'], seq, latent)
    yf, ybw, st = _scan(ops[:9], s0, seq, emit_state=not latent)
    x = _outproj(x, ada, ya, yb, yf, ybw, ops[9], sgc, p['gn_g'], p['gn_b'], p['ones_g'],
                 p['w_out'], p['g_post'], seq)
    return x, st


def _pack_states(s):
    b, nd, h, hv, hk = s.shape
    g = h // HEADS_PER_GROUP
    st = jnp.swapaxes(s, -1, -2).reshape(b, nd, g, HEADS_PER_GROUP * hk, hv)
    st = jnp.tile(st, (1, 1, 1, 1, HEADS_PER_GROUP))
    return jnp.swapaxes(st, 1, 2)


def _unpack_states(st):
    b, g, nd = st.shape[:3]
    diag = jnp.stack([st[:, :, :, h * HEAD:(h + 1) * HEAD, h * HEAD:(h + 1) * HEAD]
                      for h in range(HEADS_PER_GROUP)], axis=3)
    diag = jnp.swapaxes(diag, -1, -2)
    return jnp.swapaxes(diag, 1, 2).reshape(b, nd, g * HEADS_PER_GROUP, HEAD, HEAD)


def kernel(x_prompt, x_sample, c, state_rwkv, c_ctx, ada_w, ada_b, g_pre, g_post, w_in, conv_a,
           conv_b, conv_b_bias, ln_b_g, ln_b_b, mu, w0, w2, a0, a2, k_k, k_a, r_k, gn_g, gn_b, w_out):
    bp, seq_p, d = x_prompt.shape
    bs, seq_s, _ = x_sample.shape
    depth = ada_w.shape[0]

    rows = -(-(1 + bs) // SUBLANES) * SUBLANES
    mods = jnp.zeros((rows, d), F32).at[0].set(c_ctx).at[1:1 + bs].set(c)
    ada = _ada(mods, ada_w, ada_b)
    head_id = jnp.arange(GROUP_W) // HEAD
    ones_g = (head_id[:, None] == head_id[None, :]).astype(BF16)
    w_in_bf = w_in.astype(BF16)
    w_out_bf = w_out.astype(BF16)

    xp = x_prompt.reshape(bp * seq_p, d)
    xs = x_sample.reshape(bs * seq_s, d)
    new_states = []
    for l in range(depth):
        p = {'g_pre': g_pre[l], 'g_post': g_post[l], 'w_in': w_in_bf[l], 'conv_a': conv_a[l],
             'conv_b': conv_b[l], 'conv_b_bias': conv_b_bias[l], 'ln_b_g': ln_b_g[l],
             'ln_b_b': ln_b_b[l], 'mu': mu[l], 'w0': w0[l], 'w2': w2[l], 'a0': a0[l], 'a2': a2[l],
             'k_k': k_k[l], 'k_a': k_a[l], 'r_k': r_k[l].reshape(-1), 'gn_g': gn_g[l],
             'gn_b': gn_b[l], 'w_out': w_out_bf[l], 'ones_g': ones_g}
        xp, st = _trunk_layer(xp, ada[l, 0:1].reshape(1, 1, 3 * d), p, None, seq_p, False)
        new_states.append(_unpack_states(st))
        xs, _ = _trunk_layer(xs, ada[l, 1:1 + bs].reshape(bs, 1, 3 * d), p,
                             _pack_states(state_rwkv[:, l]), seq_s, True)
    return (xp.reshape(bp, seq_p, d), xs.reshape(bs, seq_s, d), jnp.stack(new_states, axis=1))
```

```python
import functools
import math

import jax
import jax.numpy as jnp
from jax import lax
from jax.experimental import pallas as pl
from jax.experimental.pallas import tpu as pltpu

F32 = jnp.float32
BF16 = jnp.bfloat16

SUBLANES = 8
GRID_W = 64
HEAD = 64
HEADS_PER_GROUP = 4
GROUP_W = HEAD * HEADS_PER_GROUP
CHUNK = 64
CONV_B_HALF = 15
LORA = 64
RMS_EPS = 1e-6
LN_EPS = 1e-5
GN_EPS = 64e-5
DECAY_SCALE = math.exp(-0.5)

TOK_TILE = 512
OUT_TILE = 1024
SCAN_BLOCK = 512
SCAN_SKEW = 3
VMEM_LIMIT = 56 * 1024 * 1024


def _cparams(sem):
    return pltpu.CompilerParams(dimension_semantics=sem, vmem_limit_bytes=VMEM_LIMIT)


def _layer_spec(a, layer):
    return pl.BlockSpec((1,) + a.shape[1:], lambda *_: (layer,) + (0,) * (a.ndim - 1))


def _dot(a, b):
    return jnp.dot(a.astype(BF16), b.astype(BF16), preferred_element_type=F32)


def _dot_nt(a, b):
    return lax.dot_general(a.astype(BF16), b.astype(BF16), (((1,), (1,)), ((), ())),
                           preferred_element_type=F32)


def _split3(x):
    h1 = x.astype(BF16)
    r1 = x - h1.astype(F32)
    h2 = r1.astype(BF16)
    h3 = (r1 - h2.astype(F32)).astype(BF16)
    return h1, h2, h3


def _head_sum(x, ones_g):
    parts = [_dot(x[:, g * GROUP_W:(g + 1) * GROUP_W], ones_g) for g in range(x.shape[1] // GROUP_W)]
    return jnp.concatenate(parts, axis=1)


def _silu(x):
    return x * jax.nn.sigmoid(x)


def _interleave(tasks, starts, results):
    live = list(range(len(tasks)))
    rnd = 0
    while live:
        for k in list(live):
            if rnd >= starts[k]:
                try:
                    next(tasks[k])
                except StopIteration as done:
                    results[k] = done.value
                    live.remove(k)
        rnd += 1


def _ada_kernel(mod_ref, w_ref, b_ref, o_ref):
    m = mod_ref[...]
    o_ref[0] = jnp.dot(_silu(m), w_ref[0], preferred_element_type=F32,
                       precision=lax.Precision.HIGHEST) + b_ref[0]


def _ada(mods, ada_w, ada_b):
    n_layers, d, d3 = ada_w.shape
    rows = mods.shape[0]
    blk = d
    return pl.pallas_call(
        _ada_kernel,
        grid=(n_layers, d3 // blk),
        in_specs=[pl.BlockSpec((rows, d), lambda l, j: (0, 0)),
                  pl.BlockSpec((1, d, blk), lambda l, j: (l, 0, j)),
                  pl.BlockSpec((1, 1, blk), lambda l, j: (l, 0, j))],
        out_specs=pl.BlockSpec((1, rows, blk), lambda l, j: (l, 0, j)),
        out_shape=jax.ShapeDtypeStruct((n_layers, rows, d3), F32),
        compiler_params=_cparams(("parallel", "parallel")),
        name="ada",
    )(mods, ada_w, ada_b.reshape(n_layers, 1, d3))


def _ada_spec(ada, layer, first_row, per_b):
    if per_b is None:
        index = lambda i: (layer, first_row, 0, 0)
    else:
        index = lambda i: (layer, first_row + i // per_b, 0, 0)
    return pl.BlockSpec((1, 1, 1, ada.shape[3]), index)


def _rwkv_operands(zc, prev_row, next_row, mu_ref, w0_ref, w2_ref, a0_ref, a2_ref, kk_ref, ka_ref,
                   rk_ref, ones_ref, outs, wc):
    r_o, v_o, kn_o, lw0_o, kd0_o, b0_o, lw1_o, kd1_o, b1_o, bonus_o = outs
    rows = zc.shape[0]
    ridx = lax.broadcasted_iota(jnp.int32, zc.shape, 0)
    up = jnp.where(ridx == 0, prev_row, pltpu.roll(zc, 1, 0))
    dn = jnp.where(ridx == rows - 1, next_row, pltpu.roll(zc, rows - 1, 0))
    zs = zc + mu_ref[0] * (0.5 * (up + dn) - zc)

    r, k, v = zs[:, 0:wc], zs[:, wc:2 * wc], zs[:, 2 * wc:3 * wc]
    lw = jnp.tanh(zs[:, 3 * wc:3 * wc + 2 * LORA])
    la = zs[:, 3 * wc + 2 * LORA:3 * wc + 4 * LORA]
    lane = lax.broadcasted_iota(jnp.int32, lw.shape, 1)
    ones_g = ones_ref[...]

    r_o[...] = r
    v_o[...] = v.astype(v_o.dtype)
    kk = k * kk_ref[0]
    kk_ss = _head_sum(kk * kk, ones_g)
    yield
    kn = kk * lax.rsqrt(kk_ss + 1e-12)
    kn_o[...] = kn

    kd_sum = jnp.zeros_like(k)
    for d, (lw_o, kd_o, b_o) in enumerate(((lw0_o, kd0_o, b0_o), (lw1_o, kd1_o, b1_o))):
        sel = (lane >= d * LORA) & (lane < (d + 1) * LORA)
        wl = w0_ref[0, d:d + 1, :] + _dot(jnp.where(sel, lw, 0.0), w2_ref[0])
        a = jax.nn.sigmoid(a0_ref[0, d:d + 1, :] + _dot(jnp.where(sel, la, 0.0), a2_ref[0]))
        yield
        kd = k * (1.0 + (a - 1.0) * ka_ref[0])
        lw_o[...] = -DECAY_SCALE * jax.nn.sigmoid(wl)
        kd_o[...] = kd
        b_o[...] = kn * a
        kd_sum = kd_sum + kd
    bonus_o[...] = _head_sum(r * kd_sum * rk_ref[0], ones_g) * v


def _inproj_kernel(x_ref, xp_ref, xn_ref, ada_ref, g_ref, w_ref, ca_ref, mu_ref, w0_ref, w2_ref,
                   a0_ref, a2_ref, kk_ref, ka_ref, rk_ref, ones_ref,
                   ya_ref, ub_ref, sgb_ref, sgc_ref, *c_outs, d, wa, wb, shift_w, seg, per_b):
    ada = ada_ref[0, 0]
    shift, scale = ada[:, 0:d], ada[:, d:2 * d]

    def mod_norm(x):
        y = x * lax.rsqrt(jnp.mean(x * x, -1, keepdims=True) + RMS_EPS) * g_ref[0]
        return y * (1.0 + scale) + shift

    h32 = mod_norm(x_ref[...])
    h = h32.astype(BF16)
    rows = h.shape[0]
    proj = lambda lo, hi: jnp.dot(h, w_ref[0, :, lo:hi], preferred_element_type=F32)
    off_b = 4 * wa
    off_c = off_b + 3 * wb

    h_ext = jnp.concatenate([mod_norm(xp_ref[...]), h32, mod_norm(xn_ref[...])], axis=0)
    zc_ext = jnp.dot(h_ext.astype(BF16), w_ref[0, :, off_c:off_c + shift_w],
                     preferred_element_type=F32)
    it = pl.program_id(0) % per_b
    prev_row = jnp.where(it == 0, 0.0, zc_ext[SUBLANES - 1:SUBLANES, :])
    next_row = jnp.where(it == per_b - 1, 0.0, zc_ext[SUBLANES + rows:SUBLANES + rows + 1, :])

    def branches():
        za = proj(0, off_b)
        yield
        xa, bg, cg, ga = (za[:, k * wa:(k + 1) * wa] for k in range(4))
        u = cg * xa
        pos = lax.broadcasted_iota(jnp.int32, u.shape, 0) % seg
        prev = jnp.where(pos == 0, 0.0, pltpu.roll(u, 1, 0))
        nxt = jnp.where(pos == seg - 1, 0.0, pltpu.roll(u, rows - 1, 0))
        conv = ca_ref[0, 0:1, :] * prev + ca_ref[0, 1:2, :] * u + ca_ref[0, 2:3, :] * nxt
        ya_ref[...] = (bg * conv * _silu(ga)).astype(ya_ref.dtype)
        zb = proj(off_b, off_c)
        yield
        ub_ref[...] = zb[:, 0:wb] * jax.nn.sigmoid(zb[:, wb:2 * wb])
        sgb_ref[...] = _silu(zb[:, 2 * wb:3 * wb])
        sgc_ref[...] = _silu(proj(off_c + shift_w, w_ref.shape[2]))

    tasks = [branches(),
             _rwkv_operands(zc_ext[SUBLANES:SUBLANES + rows, :], prev_row, next_row, mu_ref, w0_ref,
                            w2_ref, a0_ref, a2_ref, kk_ref, ka_ref, rk_ref, ones_ref, c_outs,
                            kk_ref.shape[2])]
    _interleave(tasks, [0, 0], [None, None])


def _inproj(x, ada, ada_row, p, layer, seq, seg):
    n, d = x.shape
    d_in = p['w_in'].shape[2]
    wa = p['conv_a'].shape[2]
    wb = p['conv_b'].shape[2]
    shift_w = p['mu'].shape[2]
    wc = p['k_k'].shape[2]
    assert d_in == 4 * wa + 3 * wb + shift_w + wc
    tile = min(TOK_TILE, seq)
    assert seq % tile == 0 and tile % seg == 0
    per_b = seq // tile
    sub = tile // SUBLANES
    n_sub = n // SUBLANES
    tok = lambda w: pl.BlockSpec((tile, w), lambda i: (i, 0))
    names = ('g_pre', 'w_in', 'conv_a', 'mu', 'w0', 'w2', 'a0', 'a2', 'k_k', 'k_a', 'r_k')
    params = [p[k] for k in names]
    outs = [(wa, BF16), (wb, F32), (wb, F32), (wc, F32)] + [
        (wc, BF16 if k == 1 else F32) for k in range(10)]
    return pl.pallas_call(
        functools.partial(_inproj_kernel, d=d, wa=wa, wb=wb, shift_w=shift_w, seg=seg, per_b=per_b),
        grid=(n // tile,),
        in_specs=[tok(d),
                  pl.BlockSpec((SUBLANES, d), lambda i: (jnp.maximum(i * sub - 1, 0), 0)),
                  pl.BlockSpec((SUBLANES, d), lambda i: (jnp.minimum((i + 1) * sub, n_sub - 1), 0)),
                  _ada_spec(ada, layer, *ada_row(per_b))]
                 + [_layer_spec(a, layer) for a in params]
                 + [pl.BlockSpec(p['ones_g'].shape, lambda i: (0, 0))],
        out_specs=[tok(w) for w, _ in outs],
        out_shape=[jax.ShapeDtypeStruct((n, w), dt) for w, dt in outs],
        compiler_params=_cparams(("parallel",)),
        name="inproj",
    )(x, x, x, ada, *params, p['ones_g'])


def _convb_finish(u, sgb, lg_ref, lb_ref):
    mu = jnp.mean(u, -1, keepdims=True)
    var = jnp.mean(jnp.square(u - mu), -1, keepdims=True)
    ln = (u - mu) * lax.rsqrt(var + LN_EPS) * lg_ref[0] + lb_ref[0]
    return _silu(ln) * sgb


def _convb_grid_kernel(u_ref, sgb_ref, w_ref, bias_ref, lg_ref, lb_ref, o_ref, *, seq):
    n_rows = seq // GRID_W
    for r in range(n_rows):
        accs = [None, None]
        for k, q in enumerate(range(max(0, r - CONV_B_HALF), min(n_rows - 1, r + CONV_B_HALF) + 1)):
            j = q - r + CONV_B_HALF
            term = w_ref[0, j:j + 1, :] * u_ref[q * GRID_W:(q + 1) * GRID_W, :]
            accs[k % 2] = term if accs[k % 2] is None else accs[k % 2] + term
        u = accs[0] + accs[1] + bias_ref[0]
        out = slice(r * GRID_W, (r + 1) * GRID_W)
        o_ref[out, :] = _convb_finish(u, sgb_ref[out, :], lg_ref, lb_ref).astype(o_ref.dtype)


def _convb_seq_kernel(u_ref, sgb_ref, w_ref, bias_ref, lg_ref, lb_ref, o_ref, pad_ref,
                      *, seq, pad_rows, row_block):
    width = pad_ref.shape[1]
    zeros = jnp.zeros((pad_rows, width), F32)
    pad_ref[0:pad_rows, :] = zeros
    pad_ref[pad_rows + seq:pad_rows + seq + pad_rows, :] = zeros
    pad_ref[pad_rows:pad_rows + seq, :] = u_ref[...]
    first = pad_rows - CONV_B_HALF
    for i in range(seq // row_block):
        start = i * row_block
        accs = [None, None]
        for j in range(2 * CONV_B_HALF + 1):
            term = w_ref[0, j:j + 1, :] * pad_ref[start + first + j:start + first + j + row_block, :]
            accs[j % 2] = term if accs[j % 2] is None else accs[j % 2] + term
        u = accs[0] + accs[1] + bias_ref[0]
        out = slice(start, start + row_block)
        o_ref[out, :] = _convb_finish(u, sgb_ref[out, :], lg_ref, lb_ref).astype(o_ref.dtype)


def _convb(ub, sgb, p, layer, seq, latent):
    n, w = ub.shape
    blk = pl.BlockSpec((seq, w), lambda b: (b, 0))
    params = [p[k] for k in ('conv_b', 'conv_b_bias', 'ln_b_g', 'ln_b_b')]
    if latent:
        body = functools.partial(_convb_grid_kernel, seq=seq)
        scratch = []
    else:
        pad_rows = -(-CONV_B_HALF // SUBLANES) * SUBLANES
        body = functools.partial(_convb_seq_kernel, seq=seq, pad_rows=pad_rows, row_block=64)
        scratch = [pltpu.VMEM((seq + 2 * pad_rows, w), F32)]
    return pl.pallas_call(
        body,
        grid=(n // seq,),
        in_specs=[blk, blk] + [_layer_spec(a, layer) for a in params],
        out_specs=blk,
        out_shape=jax.ShapeDtypeStruct((n, w), BF16),
        scratch_shapes=scratch,
        compiler_params=_cparams(("parallel",)),
        name="conv_b",
    )(ub, sgb, *params)


def _bd(y, bdmask):
    return jnp.where(bdmask, jnp.concatenate([y] * HEADS_PER_GROUP, axis=0), 0.0).astype(BF16)


def _chunk_operands(reverse, r, v, kk, logw, kd, b, masks):
    tri3, strict, incl, bdmask = masks
    c = r.shape[0]
    cs = jnp.dot(tri3, jnp.concatenate(_split3(logw), axis=0), preferred_element_type=F32)
    yield
    p = jnp.exp(cs)
    p_prev = jnp.exp(cs - logw)
    p_inv = jnp.exp(-cs)
    p_tot = p[0:1, :] if reverse else p[c - 1:c, :]
    rt = r * p
    at = -(kk * p_prev)
    bt = b * p_inv
    kt = kd * p_inv

    ar = jnp.concatenate([at, rt], axis=0)
    sb = _dot_nt(ar, _bd(bt, bdmask))
    sk = _dot_nt(ar, _bd(kt, bdmask))
    yield
    l_ab = jnp.where(strict, sb[0:c], 0.0)
    m_rb = jnp.where(incl, sb[c:2 * c], 0.0)
    m_ak = jnp.where(strict, sk[0:c], 0.0)
    m_rk = jnp.where(incl, sk[c:2 * c], 0.0)

    wv = _dot(jnp.concatenate([m_ak, m_rk], axis=0), _bd(v, bdmask))

    pw = l_ab
    out = _dot(jnp.concatenate([pw, m_rb], axis=0), _bd(pw, bdmask))
    yield
    t_off = l_ab
    mt = m_rb + out[c:2 * c]
    pw = out[0:c]
    n_steps = int(math.log2(c))
    for step in range(1, n_steps):
        lhs = [t_off, mt] if step == n_steps - 1 else [t_off, mt, pw]
        out = _dot(jnp.concatenate(lhs, axis=0), _bd(pw, bdmask))
        yield
        t_off = t_off + pw + out[0:c]
        mt = mt + out[c:2 * c]
        pw = out[2 * c:3 * c] if step < n_steps - 1 else None

    tx = _dot(jnp.concatenate([t_off, mt], axis=0),
              jnp.concatenate([_bd(at, bdmask), _bd(wv[0:c], bdmask)], axis=1))
    yield
    a_hat, w2 = at + tx[0:c, 0:GROUP_W], wv[0:c] + tx[0:c, GROUP_W:2 * GROUP_W]
    r_hat = rt + tx[c:2 * c, 0:GROUP_W]
    y0 = tx[c:2 * c, GROUP_W:2 * GROUP_W] + wv[c:2 * c]
    ar_hat = jnp.concatenate([a_hat, r_hat], axis=0).astype(BF16)
    wy = jnp.concatenate([w2, y0], axis=0)
    bk_t = jnp.concatenate([bt * p_tot, kt * p_tot], axis=0).T.astype(BF16)
    p_col = jnp.broadcast_to(p_tot, (2 * c, GROUP_W)).T
    p_col = jnp.concatenate([p_col, p_col], axis=1)
    return ar_hat, wy, bk_t, p_col, v


def _state_chain(st, fetches, y_ref, row_slices, lanes, bdmask):
    for fetch, rows in zip(fetches, row_slices):
        while fetch() is None:
            yield
        ar_hat, wy, bk_t, p_col, v = fetch()
        c = v.shape[0]
        uy = _dot(ar_hat, st) + wy
        yield
        u, y = uy[0:c], uy[c:2 * c]
        y_ref[rows, lanes] = y
        uv = jnp.concatenate([u, v.astype(F32)], axis=0)
        st = jnp.where(bdmask, st * p_col + _dot(bk_t, uv), 0.0)
        yield
    return st


def _scan_kernel(*refs, has_s0, emit_state):
    refs = list(refs)
    s0_ref = refs.pop(0) if has_s0 else None
    st_s = refs.pop()
    st_o = refs.pop() if emit_state else None
    (rf, vf, kf, lwf, kdf, bf, rb, vb, kb, lwb, kdb, bb, yf_o, yb_o) = refs
    i = pl.program_id(1)
    c = CHUNK
    n_chunks = rf.shape[0] // c
    n_groups = rf.shape[1] // GROUP_W

    row = lax.broadcasted_iota(jnp.int32, (GROUP_W, GROUP_W), 0)
    colm = lax.broadcasted_iota(jnp.int32, (GROUP_W, GROUP_W), 1)
    bdmask = (row // HEAD) == (colm // HEAD)
    t_row = lax.broadcasted_iota(jnp.int32, (c, GROUP_W), 0)
    t_col = lax.broadcasted_iota(jnp.int32, (c, GROUP_W), 1) % c
    tr = lax.broadcasted_iota(jnp.int32, (c, 3 * c), 0)
    tc = lax.broadcasted_iota(jnp.int32, (c, 3 * c), 1) % c
    masks_f = ((tc <= tr).astype(BF16), t_col < t_row, t_col <= t_row, bdmask)
    masks_b = ((tc >= tr).astype(BF16), t_col > t_row, t_col >= t_row, bdmask)

    @pl.when(i == 0)
    def _():
        if has_s0:
            for g in range(n_groups):
                for d in range(2):
                    st_s[g, d] = jnp.where(bdmask, s0_ref[0, 0, d, g], 0.0).T
        else:
            st_s[...] = jnp.zeros(st_s.shape, F32)

    rows = [pl.ds(j * c, c) for j in range(n_chunks)]
    lanes = [pl.ds(g * GROUP_W, GROUP_W) for g in range(n_groups)]
    fwd_refs = (rf, vf, kf, lwf, kdf, bf)
    bwd_refs = (rb, vb, kb, lwb, kdb, bb)
    tasks, starts = [], []
    for l in lanes:
        for j, s in enumerate(rows):
            tasks.append(_chunk_operands(False, *(ref[s, l] for ref in fwd_refs), masks_f))
            starts.append(j * SCAN_SKEW)
    for l in lanes:
        for j, s in enumerate(rows):
            tasks.append(_chunk_operands(True, *(ref[s, l] for ref in bwd_refs), masks_b))
            starts.append((n_chunks - 1 - j) * SCAN_SKEW)
    n_ops = len(tasks)
    results = [None] * (n_ops + 2 * n_groups)
    fetch = lambda k: (lambda: results[k])
    for g in range(n_groups):
        f_ids = [g * n_chunks + j for j in range(n_chunks)]
        b_ids = [(n_groups + g) * n_chunks + j for j in range(n_chunks)]
        tasks.append(_state_chain(st_s[g, 0], [fetch(k) for k in f_ids], yf_o, rows, lanes[g],
                                  bdmask))
        tasks.append(_state_chain(st_s[g, 1], [fetch(k) for k in b_ids[::-1]], yb_o, rows[::-1],
                                  lanes[g], bdmask))
        starts += [0, 0]
    _interleave(tasks, starts, results)
    for g in range(n_groups):
        for d in range(2):
            st_s[g, d] = results[n_ops + 2 * g + d]

    if emit_state:
        @pl.when(i == pl.num_programs(1) - 1)
        def _():
            for g in range(n_groups):
                for d in range(2):
                    st_o[0, d, g] = results[n_ops + 2 * g + d].T


def _scan(prep, s0, layer, seq, emit_state):
    r, v, kn, lw0, kd0, b0, lw1, kd1, b1 = prep
    n, wc = r.shape
    nb = n // seq
    ng = wc // GROUP_W
    blk = min(SCAN_BLOCK, seq)
    nt = seq // blk
    fwd = pl.BlockSpec((blk, wc), lambda b, i: (b * nt + i, 0))
    bwd = pl.BlockSpec((blk, wc), lambda b, i: (b * nt + nt - 1 - i, 0))
    ins = [r, v, kn, lw0, kd0, b0, r, v, kn, lw1, kd1, b1]
    in_specs = [fwd] * 6 + [bwd] * 6
    if s0 is not None:
        ins = [s0] + ins
        in_specs = [pl.BlockSpec((1, 1, 2, ng, GROUP_W, GROUP_W),
                                 lambda b, i: (b, layer, 0, 0, 0, 0))] + in_specs
    y_shape = jax.ShapeDtypeStruct((n, wc), F32)
    out_specs, out_shape = [fwd, bwd], [y_shape, y_shape]
    if emit_state:
        out_specs.append(pl.BlockSpec((1, 2, ng, GROUP_W, GROUP_W), lambda b, i: (b, 0, 0, 0, 0)))
        out_shape.append(jax.ShapeDtypeStruct((nb, 2, ng, GROUP_W, GROUP_W), F32))
    outs = pl.pallas_call(
        functools.partial(_scan_kernel, has_s0=s0 is not None, emit_state=emit_state),
        grid=(nb, nt),
        in_specs=in_specs,
        out_specs=out_specs,
        out_shape=out_shape,
        scratch_shapes=[pltpu.VMEM((ng, 2, GROUP_W, GROUP_W), F32)],
        compiler_params=_cparams(("parallel", "arbitrary")),
        name="rwkv_scan",
    )(*ins)
    return (outs[0], outs[1], outs[2] if emit_state else None)


def _outproj_kernel(x_ref, ada_ref, ya_ref, yb_ref, yf_ref, ybw_ref, bonus_ref, sgc_ref,
                    gng_ref, gnb_ref, w_ref, gp_ref, ones_ref, o_ref, *, d):
    ones_g = ones_ref[...]
    ys = yf_ref[...] + ybw_ref[...]
    mu = _head_sum(ys, ones_g) * (1.0 / HEAD)
    dev = ys - mu
    var = _head_sum(dev * dev, ones_g) * (1.0 / HEAD)
    yg = dev * lax.rsqrt(var + GN_EPS) * gng_ref[0] + gnb_ref[0]
    yc = (yg + bonus_ref[...]) * sgc_ref[...]
    cat = jnp.concatenate([ya_ref[...], yb_ref[...], yc.astype(BF16)], axis=-1)
    out = jnp.dot(cat, w_ref[0], preferred_element_type=F32)
    gate = ada_ref[0, 0][:, 2 * d:3 * d]
    nrm = out * lax.rsqrt(jnp.mean(out * out, -1, keepdims=True) + RMS_EPS) * gp_ref[0]
    o_ref[...] = x_ref[...] + gate * nrm


def _outproj(x, ada, ada_row, ya, yb, yf, ybw, bonus, sgc, p, layer, seq):
    n, d = x.shape
    wc = yf.shape[1]
    wa = ya.shape[1]
    tile = min(OUT_TILE, seq)
    assert seq % tile == 0
    per_b = seq // tile
    tok = lambda w: pl.BlockSpec((tile, w), lambda i: (i, 0))
    params = [p[k] for k in ('gn_g', 'gn_b', 'w_out', 'g_post')]
    return pl.pallas_call(
        functools.partial(_outproj_kernel, d=d),
        grid=(n // tile,),
        in_specs=[tok(d), _ada_spec(ada, layer, *ada_row(per_b)), tok(wa), tok(wa), tok(wc), tok(wc),
                  tok(wc), tok(wc)] + [_layer_spec(a, layer) for a in params]
                 + [pl.BlockSpec(p['ones_g'].shape, lambda i: (0, 0))],
        out_specs=tok(d),
        out_shape=jax.ShapeDtypeStruct((n, d), F32),
        compiler_params=_cparams(("parallel",)),
        name="outproj",
    )(x, ada, ya, yb, yf, ybw, bonus, sgc, *params, p['ones_g'])


def _trunk_layer(x, ada, ada_row, p, layer, s0, seq, latent):
    ya, ub, sgb, sgc, *ops = _inproj(x, ada, ada_row, p, layer, seq, GRID_W if latent else seq)
    yb = _convb(ub, sgb, p, layer, seq, latent)
    yf, ybw, st = _scan(ops[:9], s0, layer, seq, emit_state=not latent)
    x = _outproj(x, ada, ada_row, ya, yb, yf, ybw, ops[9], sgc, p, layer, seq)
    return x, st


def _unpack_states(st):
    b, nd, g = st.shape[:3]
    diag = jnp.stack([st[:, :, :, h * HEAD:(h + 1) * HEAD, h * HEAD:(h + 1) * HEAD]
                      for h in range(HEADS_PER_GROUP)], axis=3)
    return diag.reshape(b, nd, g * HEADS_PER_GROUP, HEAD, HEAD)


def kernel(x_prompt, x_sample, c, state_rwkv, c_ctx, ada_w, ada_b, g_pre, g_post, w_in, conv_a,
           conv_b, conv_b_bias, ln_b_g, ln_b_b, mu, w0, w2, a0, a2, k_k, k_a, r_k, gn_g, gn_b, w_out):
    bp, seq_p, d = x_prompt.shape
    bs, seq_s, _ = x_sample.shape
    depth = ada_w.shape[0]
    wc = k_k.shape[1]
    n_heads = wc // HEAD
    n_groups = n_heads // HEADS_PER_GROUP

    rows = -(-(1 + bs) // SUBLANES) * SUBLANES
    mods = jnp.zeros((rows, d), F32).at[0].set(c_ctx).at[1:1 + bs].set(c)
    ada = _ada(mods, ada_w, ada_b).reshape(depth, rows, 1, 3 * d)
    head_id = jnp.arange(GROUP_W) // HEAD
    row3 = lambda a: a.reshape(depth, 1, -1)
    p = {'g_pre': row3(g_pre), 'g_post': row3(g_post), 'w_in': w_in.astype(BF16), 'conv_a': conv_a,
         'conv_b': conv_b, 'conv_b_bias': row3(conv_b_bias), 'ln_b_g': row3(ln_b_g),
         'ln_b_b': row3(ln_b_b), 'mu': row3(mu), 'w0': w0, 'a0': a0,
         'w2': w2.reshape(depth, 2 * LORA, wc).astype(BF16),
         'a2': a2.reshape(depth, 2 * LORA, wc).astype(BF16),
         'k_k': row3(k_k), 'k_a': row3(k_a), 'r_k': row3(r_k), 'gn_g': row3(gn_g),
         'gn_b': row3(gn_b), 'w_out': w_out.astype(BF16),
         'ones_g': (head_id[:, None] == head_id[None, :]).astype(BF16)}
    s0 = state_rwkv.reshape(bs, depth, 2, n_groups, GROUP_W, HEAD)
    s0 = jnp.tile(s0, (1, 1, 1, 1, 1, HEADS_PER_GROUP))

    ctx_row = lambda per_b: (0, None)
    lat_row = lambda per_b: (1, per_b)
    xp = x_prompt.reshape(bp * seq_p, d)
    xs = x_sample.reshape(bs * seq_s, d)
    new_states = []
    for l in range(depth):
        xp, st = _trunk_layer(xp, ada, ctx_row, p, l, None, seq_p, False)
        new_states.append(_unpack_states(st))
        xs, _ = _trunk_layer(xs, ada, lat_row, p, l, s0, seq_s, True)
    return (xp.reshape(bp, seq_p, d), xs.reshape(bs, seq_s, d), jnp.stack(new_states, axis=1))
```

```python
import functools
import math

import jax
import jax.numpy as jnp
from jax import lax
from jax.experimental import pallas as pl
from jax.experimental.pallas import tpu as pltpu

F32 = jnp.float32
BF16 = jnp.bfloat16

SUBLANES = 8
LANES = 128
GRID_W = 64
HEAD = 64
HEADS_PER_GROUP = 4
GROUP_W = HEAD * HEADS_PER_GROUP
CHUNK = 64
CONV_B_HALF = 15
LORA = 64
RMS_EPS = 1e-6
LN_EPS = 1e-5
GN_EPS = 64e-5
DECAY_SCALE = math.exp(-0.5)

TOK_TILE = 512
OUT_TILE = 1024
SCAN_BLOCK = 512
SCAN_SKEW = 2
VMEM_LIMIT = 56 * 1024 * 1024


def _cparams(sem):
    return pltpu.CompilerParams(dimension_semantics=sem, vmem_limit_bytes=VMEM_LIMIT)


def _layer_spec(a, layer):
    return pl.BlockSpec((1,) + a.shape[1:], lambda *_: (layer,) + (0,) * (a.ndim - 1))


def _dot(a, b):
    return jnp.dot(a.astype(BF16), b.astype(BF16), preferred_element_type=F32)


def _dot_nt(a, b):
    return lax.dot_general(a.astype(BF16), b.astype(BF16), (((1,), (1,)), ((), ())),
                           preferred_element_type=F32)


def _split3(x):
    h1 = x.astype(BF16)
    r1 = x - h1.astype(F32)
    h2 = r1.astype(BF16)
    h3 = (r1 - h2.astype(F32)).astype(BF16)
    return h1, h2, h3


def _head_sum(x, ones_g):
    parts = [_dot(x[:, g * GROUP_W:(g + 1) * GROUP_W], ones_g) for g in range(x.shape[1] // GROUP_W)]
    return jnp.concatenate(parts, axis=1)


def _silu(x):
    return x * jax.nn.sigmoid(x)


def _interleave(tasks, starts, results):
    live = list(range(len(tasks)))
    rnd = 0
    while live:
        for k in list(live):
            if rnd >= starts[k]:
                try:
                    next(tasks[k])
                except StopIteration as done:
                    results[k] = done.value
                    live.remove(k)
        rnd += 1


def _ada_kernel(mod_ref, w_ref, b_ref, o_ref):
    m = mod_ref[...]
    o_ref[0] = jnp.dot(_silu(m), w_ref[0], preferred_element_type=F32,
                       precision=lax.Precision.HIGHEST) + b_ref[0]


def _ada(mods, ada_w, ada_b):
    n_layers, d, d3 = ada_w.shape
    rows = mods.shape[0]
    blk = d
    return pl.pallas_call(
        _ada_kernel,
        grid=(n_layers, d3 // blk),
        in_specs=[pl.BlockSpec((rows, d), lambda l, j: (0, 0)),
                  pl.BlockSpec((1, d, blk), lambda l, j: (l, 0, j)),
                  pl.BlockSpec((1, 1, blk), lambda l, j: (l, 0, j))],
        out_specs=pl.BlockSpec((1, rows, blk), lambda l, j: (l, 0, j)),
        out_shape=jax.ShapeDtypeStruct((n_layers, rows, d3), F32),
        compiler_params=_cparams(("parallel", "parallel")),
        name="ada",
    )(mods, ada_w, ada_b.reshape(n_layers, 1, d3))


def _ada_spec(ada, layer, first_row, per_b):
    if per_b is None:
        index = lambda i: (layer, first_row, 0, 0)
    else:
        index = lambda i: (layer, first_row + i // per_b, 0, 0)
    return pl.BlockSpec((1, 1, 1, ada.shape[3]), index)


def _rwkv_operands(zc, prev_row, next_row, first_pos, seq, mu_ref, w0_ref, w2_ref, a0_ref, a2_ref,
                   kk_ref, ka_ref, rk_ref, ones_ref, outs, wc):
    r_o, v_o, kn_o, lw0_o, kd0_o, b0_o, lw1_o, kd1_o, b1_o, bonus_o = outs
    rows = zc.shape[0]
    ridx = lax.broadcasted_iota(jnp.int32, zc.shape, 0)
    up = jnp.where(ridx == 0, prev_row, pltpu.roll(zc, 1, 0))
    dn = jnp.where(ridx == rows - 1, next_row, pltpu.roll(zc, rows - 1, 0))
    spos = (first_pos + lax.broadcasted_iota(jnp.int32, (rows, LANES), 0)) % seq
    across = lambda m: jnp.concatenate([m.astype(F32)] * (zc.shape[1] // LANES), axis=1)
    up = up * across(spos != 0)
    dn = dn * across(spos != seq - 1)
    zs = zc + mu_ref[0] * (0.5 * (up + dn) - zc)

    r, k, v = zs[:, 0:wc], zs[:, wc:2 * wc], zs[:, 2 * wc:3 * wc]
    lw = jnp.tanh(zs[:, 3 * wc:3 * wc + 2 * LORA])
    la = zs[:, 3 * wc + 2 * LORA:3 * wc + 4 * LORA]
    lane = lax.broadcasted_iota(jnp.int32, lw.shape, 1)
    ones_g = ones_ref[...]

    r_o[...] = r
    v_o[...] = v.astype(v_o.dtype)
    kk = k * kk_ref[0]
    kk_ss = _head_sum(kk * kk, ones_g)
    yield
    kn = kk * lax.rsqrt(kk_ss + 1e-12)
    kn_o[...] = kn

    kd_sum = jnp.zeros_like(k)
    for d, (lw_o, kd_o, b_o) in enumerate(((lw0_o, kd0_o, b0_o), (lw1_o, kd1_o, b1_o))):
        sel = (lane >= d * LORA) & (lane < (d + 1) * LORA)
        wl = w0_ref[0, d:d + 1, :] + _dot(jnp.where(sel, lw, 0.0), w2_ref[0])
        a = jax.nn.sigmoid(a0_ref[0, d:d + 1, :] + _dot(jnp.where(sel, la, 0.0), a2_ref[0]))
        yield
        kd = k * (1.0 + (a - 1.0) * ka_ref[0])
        lw_o[...] = -DECAY_SCALE * jax.nn.sigmoid(wl)
        kd_o[...] = kd
        b_o[...] = kn * a
        kd_sum = kd_sum + kd
    bonus_o[...] = _head_sum(r * kd_sum * rk_ref[0], ones_g) * v


def _inproj_kernel(x_ref, xp_ref, xn_ref, ada_ref, g_ref, w_ref, ca_ref, mu_ref, w0_ref, w2_ref,
                   a0_ref, a2_ref, kk_ref, ka_ref, rk_ref, ones_ref,
                   ya_ref, ub_ref, sgb_ref, sgc_ref, *c_outs, d, wa, wb, shift_w, seg, seq):
    ada = ada_ref[0, 0]
    shift, scale = ada[:, 0:d], ada[:, d:2 * d]

    def mod_norm(x):
        y = x * lax.rsqrt(jnp.mean(x * x, -1, keepdims=True) + RMS_EPS) * g_ref[0]
        return y * (1.0 + scale) + shift

    h32 = mod_norm(x_ref[...])
    h = h32.astype(BF16)
    rows = h.shape[0]
    proj = lambda lo, hi: jnp.dot(h, w_ref[0, :, lo:hi], preferred_element_type=F32)
    off_b = 4 * wa
    off_c = off_b + 3 * wb

    h_ext = jnp.concatenate([mod_norm(xp_ref[...]), h32, mod_norm(xn_ref[...])], axis=0)
    zc_ext = jnp.dot(h_ext.astype(BF16), w_ref[0, :, off_c:off_c + shift_w],
                     preferred_element_type=F32)
    prev_row = zc_ext[SUBLANES - 1:SUBLANES, :]
    next_row = zc_ext[SUBLANES + rows:SUBLANES + rows + 1, :]

    def branches():
        za = proj(0, off_b)
        yield
        xa, bg, cg, ga = (za[:, k * wa:(k + 1) * wa] for k in range(4))
        u = cg * xa
        pos = lax.broadcasted_iota(jnp.int32, u.shape, 0) % seg
        prev = jnp.where(pos == 0, 0.0, pltpu.roll(u, 1, 0))
        nxt = jnp.where(pos == seg - 1, 0.0, pltpu.roll(u, rows - 1, 0))
        conv = ca_ref[0, 0:1, :] * prev + ca_ref[0, 1:2, :] * u + ca_ref[0, 2:3, :] * nxt
        ya_ref[...] = (bg * conv * _silu(ga)).astype(ya_ref.dtype)
        zb = proj(off_b, off_c)
        yield
        ub_ref[...] = zb[:, 0:wb] * jax.nn.sigmoid(zb[:, wb:2 * wb])
        sgb_ref[...] = _silu(zb[:, 2 * wb:3 * wb])
        sgc_ref[...] = _silu(proj(off_c + shift_w, w_ref.shape[2]))

    tasks = [branches(),
             _rwkv_operands(zc_ext[SUBLANES:SUBLANES + rows, :], prev_row, next_row,
                            pl.program_id(0) * rows, seq, mu_ref, w0_ref, w2_ref, a0_ref, a2_ref,
                            kk_ref, ka_ref, rk_ref, ones_ref, c_outs, kk_ref.shape[2])]
    _interleave(tasks, [0, 0], [None, None])


def _token_tile(limit, n, seq, ada_per_seq):
    tile = min(limit, seq if ada_per_seq else n)
    assert seq % tile == 0 or tile % seq == 0
    return tile, (seq // tile if ada_per_seq else None)


def _inproj(x, ada, ada_row, p, layer, seq, seg):
    n, d = x.shape
    d_in = p['w_in'].shape[2]
    wa = p['conv_a'].shape[2]
    wb = p['conv_b'].shape[2]
    shift_w = p['mu'].shape[2]
    wc = p['k_k'].shape[2]
    assert d_in == 4 * wa + 3 * wb + shift_w + wc
    tile, per_b = _token_tile(TOK_TILE, n, seq, ada_row[1])
    assert tile % seg == 0
    sub = tile // SUBLANES
    n_sub = n // SUBLANES
    tok = lambda w: pl.BlockSpec((tile, w), lambda i: (i, 0))
    names = ('g_pre', 'w_in', 'conv_a', 'mu', 'w0', 'w2', 'a0', 'a2', 'k_k', 'k_a', 'r_k')
    params = [p[k] for k in names]
    outs = [(wa, BF16), (wb, F32), (wb, F32), (wc, F32)] + [
        (wc, BF16 if k == 1 else F32) for k in range(10)]
    return pl.pallas_call(
        functools.partial(_inproj_kernel, d=d, wa=wa, wb=wb, shift_w=shift_w, seg=seg, seq=seq),
        grid=(n // tile,),
        in_specs=[tok(d),
                  pl.BlockSpec((SUBLANES, d), lambda i: (jnp.maximum(i * sub - 1, 0), 0)),
                  pl.BlockSpec((SUBLANES, d), lambda i: (jnp.minimum((i + 1) * sub, n_sub - 1), 0)),
                  _ada_spec(ada, layer, ada_row[0], per_b)]
                 + [_layer_spec(a, layer) for a in params]
                 + [pl.BlockSpec(p['ones_g'].shape, lambda i: (0, 0))],
        out_specs=[tok(w) for w, _ in outs],
        out_shape=[jax.ShapeDtypeStruct((n, w), dt) for w, dt in outs],
        compiler_params=_cparams(("parallel",)),
        name="inproj",
    )(x, x, x, ada, *params, p['ones_g'])


def _convb_finish(u, sgb, lg_ref, lb_ref):
    mu = jnp.mean(u, -1, keepdims=True)
    var = jnp.mean(jnp.square(u - mu), -1, keepdims=True)
    ln = (u - mu) * lax.rsqrt(var + LN_EPS) * lg_ref[0] + lb_ref[0]
    return _silu(ln) * sgb


def _convb_grid_kernel(u_ref, sgb_ref, w_ref, bias_ref, lg_ref, lb_ref, o_ref, *, seq):
    n_rows = seq // GRID_W
    for r in range(n_rows):
        accs = [None, None]
        for k, q in enumerate(range(max(0, r - CONV_B_HALF), min(n_rows - 1, r + CONV_B_HALF) + 1)):
            j = q - r + CONV_B_HALF
            term = w_ref[0, j:j + 1, :] * u_ref[q * GRID_W:(q + 1) * GRID_W, :]
            accs[k % 2] = term if accs[k % 2] is None else accs[k % 2] + term
        u = accs[0] + accs[1] + bias_ref[0]
        out = slice(r * GRID_W, (r + 1) * GRID_W)
        o_ref[out, :] = _convb_finish(u, sgb_ref[out, :], lg_ref, lb_ref).astype(o_ref.dtype)


def _convb_seq_kernel(u_ref, sgb_ref, w_ref, bias_ref, lg_ref, lb_ref, o_ref, pad_ref,
                      *, seq, pad_rows, row_block):
    width = pad_ref.shape[1]
    zeros = jnp.zeros((pad_rows, width), F32)
    pad_ref[0:pad_rows, :] = zeros
    pad_ref[pad_rows + seq:pad_rows + seq + pad_rows, :] = zeros
    pad_ref[pad_rows:pad_rows + seq, :] = u_ref[...]
    first = pad_rows - CONV_B_HALF
    for i in range(seq // row_block):
        start = i * row_block
        accs = [None, None]
        for j in range(2 * CONV_B_HALF + 1):
            term = w_ref[0, j:j + 1, :] * pad_ref[start + first + j:start + first + j + row_block, :]
            accs[j % 2] = term if accs[j % 2] is None else accs[j % 2] + term
        u = accs[0] + accs[1] + bias_ref[0]
        out = slice(start, start + row_block)
        o_ref[out, :] = _convb_finish(u, sgb_ref[out, :], lg_ref, lb_ref).astype(o_ref.dtype)


def _convb(ub, sgb, p, layer, seq, latent):
    n, w = ub.shape
    blk = pl.BlockSpec((seq, w), lambda b: (b, 0))
    params = [p[k] for k in ('conv_b', 'conv_b_bias', 'ln_b_g', 'ln_b_b')]
    if latent:
        body = functools.partial(_convb_grid_kernel, seq=seq)
        scratch = []
    else:
        pad_rows = -(-CONV_B_HALF // SUBLANES) * SUBLANES
        body = functools.partial(_convb_seq_kernel, seq=seq, pad_rows=pad_rows, row_block=64)
        scratch = [pltpu.VMEM((seq + 2 * pad_rows, w), F32)]
    return pl.pallas_call(
        body,
        grid=(n // seq,),
        in_specs=[blk, blk] + [_layer_spec(a, layer) for a in params],
        out_specs=blk,
        out_shape=jax.ShapeDtypeStruct((n, w), BF16),
        scratch_shapes=scratch,
        compiler_params=_cparams(("parallel",)),
        name="conv_b",
    )(ub, sgb, *params)


def _bd(y, bdmask):
    return jnp.where(bdmask, jnp.concatenate([y] * HEADS_PER_GROUP, axis=0), 0.0).astype(BF16)


def _chunk_operands(reverse, r, v, kk, logw, kd, b, masks):
    tri3, strict, incl, bdmask = masks
    c = r.shape[0]
    cs = jnp.dot(tri3, jnp.concatenate(_split3(logw), axis=0), preferred_element_type=F32)
    yield
    p = jnp.exp(cs)
    p_prev = jnp.exp(cs - logw)
    p_inv = jnp.exp(-cs)
    p_tot = p[0:1, :] if reverse else p[c - 1:c, :]
    rt = r * p
    at = -(kk * p_prev)
    bt = b * p_inv
    kt = kd * p_inv

    ar = jnp.concatenate([at, rt], axis=0)
    sb = _dot_nt(ar, _bd(bt, bdmask))
    sk = _dot_nt(ar, _bd(kt, bdmask))
    yield
    l_ab = jnp.where(strict, sb[0:c], 0.0)
    m_rb = jnp.where(incl, sb[c:2 * c], 0.0)
    m_ak = jnp.where(strict, sk[0:c], 0.0)
    m_rk = jnp.where(incl, sk[c:2 * c], 0.0)

    wv = _dot(jnp.concatenate([m_ak, m_rk], axis=0), _bd(v, bdmask))

    pw = l_ab
    out = _dot(jnp.concatenate([pw, m_rb], axis=0), _bd(pw, bdmask))
    yield
    t_off = l_ab
    mt = m_rb + out[c:2 * c]
    pw = out[0:c]
    n_steps = int(math.log2(c))
    for step in range(1, n_steps):
        lhs = [t_off, mt] if step == n_steps - 1 else [t_off, mt, pw]
        out = _dot(jnp.concatenate(lhs, axis=0), _bd(pw, bdmask))
        yield
        t_off = t_off + pw + out[0:c]
        mt = mt + out[c:2 * c]
        pw = out[2 * c:3 * c] if step < n_steps - 1 else None

    tx = _dot(jnp.concatenate([t_off, mt], axis=0),
              jnp.concatenate([_bd(at, bdmask), _bd(wv[0:c], bdmask)], axis=1))
    yield
    a_hat, w2 = at + tx[0:c, 0:GROUP_W], wv[0:c] + tx[0:c, GROUP_W:2 * GROUP_W]
    r_hat = rt + tx[c:2 * c, 0:GROUP_W]
    y0 = tx[c:2 * c, GROUP_W:2 * GROUP_W] + wv[c:2 * c]
    ar_hat = jnp.concatenate([a_hat, r_hat], axis=0).astype(BF16)
    wy = jnp.concatenate([w2, y0], axis=0)
    bk_t = jnp.concatenate([bt * p_tot, kt * p_tot], axis=0).T.astype(BF16)
    p_col = jnp.broadcast_to(p_tot, (2 * c, GROUP_W)).T
    p_col = jnp.concatenate([p_col, p_col], axis=1)
    return ar_hat, wy, bk_t, p_col, v


def _state_chain(st, fetches, y_ref, row_slices, lanes, bdmask):
    for fetch, rows in zip(fetches, row_slices):
        while fetch() is None:
            yield
        ar_hat, wy, bk_t, p_col, v = fetch()
        c = v.shape[0]
        uy = _dot(ar_hat, st) + wy
        yield
        u, y = uy[0:c], uy[c:2 * c]
        y_ref[rows, lanes] = y
        uv = jnp.concatenate([u, v.astype(F32)], axis=0)
        st = jnp.where(bdmask, st * p_col + _dot(bk_t, uv), 0.0)
        yield
    return st


def _scan_kernel(*refs, has_s0, emit_state):
    refs = list(refs)
    s0_ref = refs.pop(0) if has_s0 else None
    st_s = refs.pop()
    st_o = refs.pop() if emit_state else None
    (rf, vf, kf, lwf, kdf, bf, rb, vb, kb, lwb, kdb, bb, yf_o, yb_o) = refs
    i = pl.program_id(1)
    c = CHUNK
    n_chunks = rf.shape[0] // c
    n_groups = rf.shape[1] // GROUP_W

    row = lax.broadcasted_iota(jnp.int32, (GROUP_W, GROUP_W), 0)
    colm = lax.broadcasted_iota(jnp.int32, (GROUP_W, GROUP_W), 1)
    bdmask = (row // HEAD) == (colm // HEAD)
    t_row = lax.broadcasted_iota(jnp.int32, (c, GROUP_W), 0)
    t_col = lax.broadcasted_iota(jnp.int32, (c, GROUP_W), 1) % c
    tr = lax.broadcasted_iota(jnp.int32, (c, 3 * c), 0)
    tc = lax.broadcasted_iota(jnp.int32, (c, 3 * c), 1) % c
    masks_f = ((tc <= tr).astype(BF16), t_col < t_row, t_col <= t_row, bdmask)
    masks_b = ((tc >= tr).astype(BF16), t_col > t_row, t_col >= t_row, bdmask)

    @pl.when(i == 0)
    def _():
        if has_s0:
            for g in range(n_groups):
                for d in range(2):
                    st_s[g, d] = jnp.where(bdmask, s0_ref[0, 0, d, g], 0.0).T
        else:
            st_s[...] = jnp.zeros(st_s.shape, F32)

    rows = [pl.ds(j * c, c) for j in range(n_chunks)]
    lanes = [pl.ds(g * GROUP_W, GROUP_W) for g in range(n_groups)]
    fwd_refs = (rf, vf, kf, lwf, kdf, bf)
    bwd_refs = (rb, vb, kb, lwb, kdb, bb)
    tasks, starts = [], []
    for l in lanes:
        for j, s in enumerate(rows):
            tasks.append(_chunk_operands(False, *(ref[s, l] for ref in fwd_refs), masks_f))
            starts.append(j * SCAN_SKEW)
    for l in lanes:
        for j, s in enumerate(rows):
            tasks.append(_chunk_operands(True, *(ref[s, l] for ref in bwd_refs), masks_b))
            starts.append((n_chunks - 1 - j) * SCAN_SKEW)
    n_ops = len(tasks)
    results = [None] * (n_ops + 2 * n_groups)
    fetch = lambda k: (lambda: results[k])
    for g in range(n_groups):
        f_ids = [g * n_chunks + j for j in range(n_chunks)]
        b_ids = [(n_groups + g) * n_chunks + j for j in range(n_chunks)]
        tasks.append(_state_chain(st_s[g, 0], [fetch(k) for k in f_ids], yf_o, rows, lanes[g],
                                  bdmask))
        tasks.append(_state_chain(st_s[g, 1], [fetch(k) for k in b_ids[::-1]], yb_o, rows[::-1],
                                  lanes[g], bdmask))
        starts += [0, 0]
    _interleave(tasks, starts, results)
    for g in range(n_groups):
        for d in range(2):
            st_s[g, d] = results[n_ops + 2 * g + d]

    if emit_state:
        @pl.when(i == pl.num_programs(1) - 1)
        def _():
            for g in range(n_groups):
                for d in range(2):
                    st_o[0, d, g] = results[n_ops + 2 * g + d].T


def _scan(prep, s0, layer, seq, emit_state):
    r, v, kn, lw0, kd0, b0, lw1, kd1, b1 = prep
    n, wc = r.shape
    nb = n // seq
    ng = wc // GROUP_W
    blk = min(SCAN_BLOCK, seq)
    nt = seq // blk
    fwd = pl.BlockSpec((blk, wc), lambda b, i: (b * nt + i, 0))
    bwd = pl.BlockSpec((blk, wc), lambda b, i: (b * nt + nt - 1 - i, 0))
    ins = [r, v, kn, lw0, kd0, b0, r, v, kn, lw1, kd1, b1]
    in_specs = [fwd] * 6 + [bwd] * 6
    if s0 is not None:
        ins = [s0] + ins
        in_specs = [pl.BlockSpec((1, 1, 2, ng, GROUP_W, GROUP_W),
                                 lambda b, i: (b, layer, 0, 0, 0, 0))] + in_specs
    y_shape = jax.ShapeDtypeStruct((n, wc), F32)
    out_specs, out_shape = [fwd, bwd], [y_shape, y_shape]
    if emit_state:
        out_specs.append(pl.BlockSpec((1, 2, ng, GROUP_W, GROUP_W), lambda b, i: (b, 0, 0, 0, 0)))
        out_shape.append(jax.ShapeDtypeStruct((nb, 2, ng, GROUP_W, GROUP_W), F32))
    outs = pl.pallas_call(
        functools.partial(_scan_kernel, has_s0=s0 is not None, emit_state=emit_state),
        grid=(nb, nt),
        in_specs=in_specs,
        out_specs=out_specs,
        out_shape=out_shape,
        scratch_shapes=[pltpu.VMEM((ng, 2, GROUP_W, GROUP_W), F32)],
        compiler_params=_cparams(("parallel", "arbitrary")),
        name="rwkv_scan",
    )(*ins)
    return (outs[0], outs[1], outs[2] if emit_state else None)


def _outproj_kernel(x_ref, ada_ref, ya_ref, yb_ref, yf_ref, ybw_ref, bonus_ref, sgc_ref,
                    gng_ref, gnb_ref, w_ref, gp_ref, ones_ref, o_ref, *, d):
    ones_g = ones_ref[...]
    ys = yf_ref[...] + ybw_ref[...]
    mu = _head_sum(ys, ones_g) * (1.0 / HEAD)
    dev = ys - mu
    var = _head_sum(dev * dev, ones_g) * (1.0 / HEAD)
    yg = dev * lax.rsqrt(var + GN_EPS) * gng_ref[0] + gnb_ref[0]
    yc = (yg + bonus_ref[...]) * sgc_ref[...]
    cat = jnp.concatenate([ya_ref[...], yb_ref[...], yc.astype(BF16)], axis=-1)
    out = jnp.dot(cat, w_ref[0], preferred_element_type=F32)
    gate = ada_ref[0, 0][:, 2 * d:3 * d]
    nrm = out * lax.rsqrt(jnp.mean(out * out, -1, keepdims=True) + RMS_EPS) * gp_ref[0]
    o_ref[...] = x_ref[...] + gate * nrm


def _outproj(x, ada, ada_row, ya, yb, yf, ybw, bonus, sgc, p, layer, seq):
    n, d = x.shape
    wc = yf.shape[1]
    wa = ya.shape[1]
    tile, per_b = _token_tile(OUT_TILE, n, seq, ada_row[1])
    tok = lambda w: pl.BlockSpec((tile, w), lambda i: (i, 0))
    params = [p[k] for k in ('gn_g', 'gn_b', 'w_out', 'g_post')]
    return pl.pallas_call(
        functools.partial(_outproj_kernel, d=d),
        grid=(n // tile,),
        in_specs=[tok(d), _ada_spec(ada, layer, ada_row[0], per_b), tok(wa), tok(wa), tok(wc), tok(wc),
                  tok(wc), tok(wc)] + [_layer_spec(a, layer) for a in params]
                 + [pl.BlockSpec(p['ones_g'].shape, lambda i: (0, 0))],
        out_specs=tok(d),
        out_shape=jax.ShapeDtypeStruct((n, d), F32),
        compiler_params=_cparams(("parallel",)),
        name="outproj",
    )(x, ada, ya, yb, yf, ybw, bonus, sgc, *params, p['ones_g'])


def _trunk_layer(x, ada, ada_row, p, layer, s0, seq, latent):
    ya, ub, sgb, sgc, *ops = _inproj(x, ada, ada_row, p, layer, seq, GRID_W if latent else seq)
    yb = _convb(ub, sgb, p, layer, seq, latent)
    yf, ybw, st = _scan(ops[:9], s0, layer, seq, emit_state=not latent)
    x = _outproj(x, ada, ada_row, ya, yb, yf, ybw, ops[9], sgc, p, layer, seq)
    return x, st


def _unpack_states(st):
    b, nd, g = st.shape[:3]
    diag = jnp.stack([st[:, :, :, h * HEAD:(h + 1) * HEAD, h * HEAD:(h + 1) * HEAD]
                      for h in range(HEADS_PER_GROUP)], axis=3)
    return diag.reshape(b, nd, g * HEADS_PER_GROUP, HEAD, HEAD)


def kernel(x_prompt, x_sample, c, state_rwkv, c_ctx, ada_w, ada_b, g_pre, g_post, w_in, conv_a,
           conv_b, conv_b_bias, ln_b_g, ln_b_b, mu, w0, w2, a0, a2, k_k, k_a, r_k, gn_g, gn_b, w_out):
    bp, seq_p, d = x_prompt.shape
    bs, seq_s, _ = x_sample.shape
    depth = ada_w.shape[0]
    wc = k_k.shape[1]
    n_heads = wc // HEAD
    n_groups = n_heads // HEADS_PER_GROUP

    rows = -(-(1 + bs) // SUBLANES) * SUBLANES
    mods = jnp.zeros((rows, d), F32).at[0].set(c_ctx).at[1:1 + bs].set(c)
    ada = _ada(mods, ada_w, ada_b).reshape(depth, rows, 1, 3 * d)
    head_id = jnp.arange(GROUP_W) // HEAD
    row3 = lambda a: a.reshape(depth, 1, -1)
    p = {'g_pre': row3(g_pre), 'g_post': row3(g_post), 'w_in': w_in.astype(BF16), 'conv_a': conv_a,
         'conv_b': conv_b, 'conv_b_bias': row3(conv_b_bias), 'ln_b_g': row3(ln_b_g),
         'ln_b_b': row3(ln_b_b), 'mu': row3(mu), 'w0': w0, 'a0': a0,
         'w2': w2.reshape(depth, 2 * LORA, wc).astype(BF16),
         'a2': a2.reshape(depth, 2 * LORA, wc).astype(BF16),
         'k_k': row3(k_k), 'k_a': row3(k_a), 'r_k': row3(r_k), 'gn_g': row3(gn_g),
         'gn_b': row3(gn_b), 'w_out': w_out.astype(BF16),
         'ones_g': (head_id[:, None] == head_id[None, :]).astype(BF16)}
    s0 = state_rwkv.reshape(bs, depth, 2, n_groups, GROUP_W, HEAD)
    s0 = jnp.tile(s0, (1, 1, 1, 1, 1, HEADS_PER_GROUP))

    ctx_row = (0, False)
    lat_row = (1, True)
    xp = x_prompt.reshape(bp * seq_p, d)
    xs = x_sample.reshape(bs * seq_s, d)
    new_states = []
    for l in range(depth):
        xp, st = _trunk_layer(xp, ada, ctx_row, p, l, None, seq_p, False)
        new_states.append(_unpack_states(st))
        xs, _ = _trunk_layer(xs, ada, lat_row, p, l, s0, seq_s, True)
    return (xp.reshape(bp, seq_p, d), xs.reshape(bs, seq_s, d), jnp.stack(new_states, axis=1))
```

```python
import functools
import math

import jax
import jax.numpy as jnp
from jax import lax
from jax.experimental import pallas as pl
from jax.experimental.pallas import tpu as pltpu

F32 = jnp.float32
BF16 = jnp.bfloat16

SUBLANES = 8
LANES = 128
GRID_W = 64
HEAD = 64
HEADS_PER_GROUP = 4
GROUP_W = HEAD * HEADS_PER_GROUP
CHUNK = 64
CONV_B_HALF = 15
LORA = 64
RMS_EPS = 1e-6
LN_EPS = 1e-5
GN_EPS = 64e-5
DECAY_SCALE = math.exp(-0.5)

TOK_TILE = 512
OUT_TILE = 1024
SCAN_BLOCK = 512
SCAN_SKEW = 2
VMEM_LIMIT = 56 * 1024 * 1024


def _cparams(sem):
    return pltpu.CompilerParams(dimension_semantics=sem, vmem_limit_bytes=VMEM_LIMIT)


def _layer_spec(a, layer):
    return pl.BlockSpec((1,) + a.shape[1:], lambda *_: (layer,) + (0,) * (a.ndim - 1))


def _dot(a, b):
    return jnp.dot(a.astype(BF16), b.astype(BF16), preferred_element_type=F32)


def _dot_nt(a, b):
    return lax.dot_general(a.astype(BF16), b.astype(BF16), (((1,), (1,)), ((), ())),
                           preferred_element_type=F32)


def _split3(x):
    h1 = x.astype(BF16)
    r1 = x - h1.astype(F32)
    h2 = r1.astype(BF16)
    h3 = (r1 - h2.astype(F32)).astype(BF16)
    return h1, h2, h3


def _head_sum(x, ones_g):
    parts = [_dot(x[:, g * GROUP_W:(g + 1) * GROUP_W], ones_g) for g in range(x.shape[1] // GROUP_W)]
    return jnp.concatenate(parts, axis=1)


def _silu(x):
    return x * jax.nn.sigmoid(x)


def _interleave(tasks, starts, results):
    live = list(range(len(tasks)))
    rnd = 0
    while live:
        for k in list(live):
            if rnd >= starts[k]:
                try:
                    next(tasks[k])
                except StopIteration as done:
                    results[k] = done.value
                    live.remove(k)
        rnd += 1


def _ada_kernel(mod_ref, w_ref, b_ref, o_ref):
    m = mod_ref[...]
    o_ref[0] = jnp.dot(_silu(m), w_ref[0], preferred_element_type=F32,
                       precision=lax.Precision.HIGHEST) + b_ref[0]


def _ada(mods, ada_w, ada_b):
    n_layers, d, d3 = ada_w.shape
    rows = mods.shape[0]
    blk = d
    return pl.pallas_call(
        _ada_kernel,
        grid=(n_layers, d3 // blk),
        in_specs=[pl.BlockSpec((rows, d), lambda l, j: (0, 0)),
                  pl.BlockSpec((1, d, blk), lambda l, j: (l, 0, j)),
                  pl.BlockSpec((1, 1, blk), lambda l, j: (l, 0, j))],
        out_specs=pl.BlockSpec((1, rows, blk), lambda l, j: (l, 0, j)),
        out_shape=jax.ShapeDtypeStruct((n_layers, rows, d3), F32),
        compiler_params=_cparams(("parallel", "parallel")),
        name="ada",
    )(mods, ada_w, ada_b.reshape(n_layers, 1, d3))


def _ada_spec(ada, layer, first_row, per_b):
    if per_b is None:
        index = lambda i: (layer, first_row, 0, 0)
    else:
        index = lambda i: (layer, first_row + i // per_b, 0, 0)
    return pl.BlockSpec((1, 1, 1, ada.shape[3]), index)


def _rwkv_operands(zc, prev_row, next_row, first_pos, seq, mu_ref, w0_ref, w2_ref, a0_ref, a2_ref,
                   kk_ref, ka_ref, rk_ref, ones_ref, outs, wc):
    r_o, v_o, kn_o, lw0_o, kd0_o, b0_o, lw1_o, kd1_o, b1_o, bonus_o = outs
    rows = zc.shape[0]
    ridx = lax.broadcasted_iota(jnp.int32, zc.shape, 0)
    up = jnp.where(ridx == 0, prev_row, pltpu.roll(zc, 1, 0))
    dn = jnp.where(ridx == rows - 1, next_row, pltpu.roll(zc, rows - 1, 0))
    spos = (first_pos + lax.broadcasted_iota(jnp.int32, (rows, LANES), 0)) % seq
    across = lambda m: jnp.concatenate([m.astype(F32)] * (zc.shape[1] // LANES), axis=1)
    up = up * across(spos != 0)
    dn = dn * across(spos != seq - 1)
    zs = zc + mu_ref[0] * (0.5 * (up + dn) - zc)

    r, k, v = zs[:, 0:wc], zs[:, wc:2 * wc], zs[:, 2 * wc:3 * wc]
    lw = jnp.tanh(zs[:, 3 * wc:3 * wc + 2 * LORA])
    la = zs[:, 3 * wc + 2 * LORA:3 * wc + 4 * LORA]
    lane = lax.broadcasted_iota(jnp.int32, lw.shape, 1)
    ones_g = ones_ref[...]

    r_o[...] = r
    v_o[...] = v.astype(v_o.dtype)
    kk = k * kk_ref[0]
    kk_ss = _head_sum(kk * kk, ones_g)
    yield
    kn = kk * lax.rsqrt(kk_ss + 1e-12)
    kn_o[...] = kn

    kd_sum = jnp.zeros_like(k)
    for d, (lw_o, kd_o, b_o) in enumerate(((lw0_o, kd0_o, b0_o), (lw1_o, kd1_o, b1_o))):
        sel = (lane >= d * LORA) & (lane < (d + 1) * LORA)
        wl = w0_ref[0, d:d + 1, :] + _dot(jnp.where(sel, lw, 0.0), w2_ref[0])
        a = jax.nn.sigmoid(a0_ref[0, d:d + 1, :] + _dot(jnp.where(sel, la, 0.0), a2_ref[0]))
        yield
        kd = k * (1.0 + (a - 1.0) * ka_ref[0])
        lw_o[...] = -DECAY_SCALE * jax.nn.sigmoid(wl)
        kd_o[...] = kd
        b_o[...] = kn * a
        kd_sum = kd_sum + kd
    bonus_o[...] = _head_sum(r * kd_sum * rk_ref[0], ones_g) * v


def _inproj_kernel(x_ref, xp_ref, xn_ref, ada_ref, g_ref, w_ref, ca_ref, mu_ref, w0_ref, w2_ref,
                   a0_ref, a2_ref, kk_ref, ka_ref, rk_ref, ones_ref,
                   ya_ref, ub_ref, sgb_ref, sgc_ref, *c_outs, d, wa, wb, shift_w, seg, seq):
    ada = ada_ref[0, 0]
    shift, scale = ada[:, 0:d], ada[:, d:2 * d]

    def mod_norm(x):
        y = x * lax.rsqrt(jnp.mean(x * x, -1, keepdims=True) + RMS_EPS) * g_ref[0]
        return y * (1.0 + scale) + shift

    h32 = mod_norm(x_ref[...])
    h = h32.astype(BF16)
    rows = h.shape[0]
    proj = lambda lo, hi: jnp.dot(h, w_ref[0, :, lo:hi], preferred_element_type=F32)
    off_b = 4 * wa
    off_c = off_b + 3 * wb

    h_ext = jnp.concatenate([mod_norm(xp_ref[...]), h32, mod_norm(xn_ref[...])], axis=0)
    zc_ext = jnp.dot(h_ext.astype(BF16), w_ref[0, :, off_c:off_c + shift_w],
                     preferred_element_type=F32)
    prev_row = zc_ext[SUBLANES - 1:SUBLANES, :]
    next_row = zc_ext[SUBLANES + rows:SUBLANES + rows + 1, :]

    def branches():
        za = proj(0, off_b)
        yield
        xa, bg, cg, ga = (za[:, k * wa:(k + 1) * wa] for k in range(4))
        u = cg * xa
        pos = lax.broadcasted_iota(jnp.int32, u.shape, 0) % seg
        prev = jnp.where(pos == 0, 0.0, pltpu.roll(u, 1, 0))
        nxt = jnp.where(pos == seg - 1, 0.0, pltpu.roll(u, rows - 1, 0))
        conv = ca_ref[0, 0:1, :] * prev + ca_ref[0, 1:2, :] * u + ca_ref[0, 2:3, :] * nxt
        ya_ref[...] = (bg * conv * _silu(ga)).astype(ya_ref.dtype)
        zb = proj(off_b, off_c)
        yield
        ub_ref[...] = zb[:, 0:wb] * jax.nn.sigmoid(zb[:, wb:2 * wb])
        sgb_ref[...] = _silu(zb[:, 2 * wb:3 * wb])
        sgc_ref[...] = _silu(proj(off_c + shift_w, w_ref.shape[2]))

    tasks = [branches(),
             _rwkv_operands(zc_ext[SUBLANES:SUBLANES + rows, :], prev_row, next_row,
                            pl.program_id(0) * rows, seq, mu_ref, w0_ref, w2_ref, a0_ref, a2_ref,
                            kk_ref, ka_ref, rk_ref, ones_ref, c_outs, kk_ref.shape[2])]
    _interleave(tasks, [0, 0], [None, None])


def _token_tile(limit, n, seq, ada_per_seq):
    tile = min(limit, seq if ada_per_seq else n)
    assert seq % tile == 0 or tile % seq == 0
    return tile, (seq // tile if ada_per_seq else None)


def _inproj(x, ada, ada_row, p, layer, seq, seg):
    n, d = x.shape
    d_in = p['w_in'].shape[2]
    wa = p['conv_a'].shape[2]
    wb = p['conv_b'].shape[2]
    shift_w = p['mu'].shape[2]
    wc = p['k_k'].shape[2]
    assert d_in == 4 * wa + 3 * wb + shift_w + wc
    tile, per_b = _token_tile(TOK_TILE, n, seq, ada_row[1])
    assert tile % seg == 0
    sub = tile // SUBLANES
    n_sub = n // SUBLANES
    tok = lambda w: pl.BlockSpec((tile, w), lambda i: (i, 0))
    names = ('g_pre', 'w_in', 'conv_a', 'mu', 'w0', 'w2', 'a0', 'a2', 'k_k', 'k_a', 'r_k')
    params = [p[k] for k in names]
    outs = [(wa, BF16), (wb, F32), (wb, F32), (wc, F32)] + [
        (wc, BF16 if k == 1 else F32) for k in range(10)]
    return pl.pallas_call(
        functools.partial(_inproj_kernel, d=d, wa=wa, wb=wb, shift_w=shift_w, seg=seg, seq=seq),
        grid=(n // tile,),
        in_specs=[tok(d),
                  pl.BlockSpec((SUBLANES, d), lambda i: (jnp.maximum(i * sub - 1, 0), 0)),
                  pl.BlockSpec((SUBLANES, d), lambda i: (jnp.minimum((i + 1) * sub, n_sub - 1), 0)),
                  _ada_spec(ada, layer, ada_row[0], per_b)]
                 + [_layer_spec(a, layer) for a in params]
                 + [pl.BlockSpec(p['ones_g'].shape, lambda i: (0, 0))],
        out_specs=[tok(w) for w, _ in outs],
        out_shape=[jax.ShapeDtypeStruct((n, w), dt) for w, dt in outs],
        compiler_params=_cparams(("parallel",)),
        name="inproj",
    )(x, x, x, ada, *params, p['ones_g'])


def _convb_finish(u, sgb, lg_ref, lb_ref):
    mu = jnp.mean(u, -1, keepdims=True)
    var = jnp.mean(jnp.square(u - mu), -1, keepdims=True)
    ln = (u - mu) * lax.rsqrt(var + LN_EPS) * lg_ref[0] + lb_ref[0]
    return _silu(ln) * sgb


def _convb_grid_rows(u_ref, sgb_ref, w_ref, bias_ref, lg_ref, lb_ref, yb_ref, *, first_row, n_out,
                     n_rows):
    for k in range(n_out):
        r = first_row + k
        accs = [None, None]
        for m, q in enumerate(range(max(0, r - CONV_B_HALF), min(n_rows - 1, r + CONV_B_HALF) + 1)):
            j = q - r + CONV_B_HALF
            term = w_ref[0, j:j + 1, :] * u_ref[q * GRID_W:(q + 1) * GRID_W, :]
            accs[m % 2] = term if accs[m % 2] is None else accs[m % 2] + term
        u = accs[0] + accs[1] + bias_ref[0]
        out = slice(k * GRID_W, (k + 1) * GRID_W)
        yb_ref[out, :] = _convb_finish(u, sgb_ref[out, :], lg_ref, lb_ref).astype(yb_ref.dtype)


def _convb_sequences(u_ref, sgb_ref, w_ref, bias_ref, lg_ref, lb_ref, yb_ref, pad_ref, *, seq,
                     pad_rows, row_block):
    zeros = jnp.zeros((pad_rows, pad_ref.shape[2]), F32)
    first = pad_rows - CONV_B_HALF
    for s in range(pad_ref.shape[0]):
        pad_ref[s, 0:pad_rows, :] = zeros
        pad_ref[s, pad_rows + seq:pad_rows + seq + pad_rows, :] = zeros
        pad_ref[s, pad_rows:pad_rows + seq, :] = u_ref[s * seq:(s + 1) * seq, :]
        for i in range(seq // row_block):
            start = i * row_block
            accs = [None, None]
            for j in range(2 * CONV_B_HALF + 1):
                lo = start + first + j
                term = w_ref[0, j:j + 1, :] * pad_ref[s, lo:lo + row_block, :]
                accs[j % 2] = term if accs[j % 2] is None else accs[j % 2] + term
            u = accs[0] + accs[1] + bias_ref[0]
            out = slice(s * seq + start, s * seq + start + row_block)
            yb_ref[out, :] = _convb_finish(u, sgb_ref[out, :], lg_ref, lb_ref).astype(yb_ref.dtype)


def _bd(y, bdmask):
    return jnp.where(bdmask, jnp.concatenate([y] * HEADS_PER_GROUP, axis=0), 0.0).astype(BF16)


def _chunk_operands(reverse, r, v, kk, logw, kd, b, masks):
    tri3, strict, incl, bdmask = masks
    c = r.shape[0]
    cs = jnp.dot(tri3, jnp.concatenate(_split3(logw), axis=0), preferred_element_type=F32)
    yield
    p = jnp.exp(cs)
    p_prev = jnp.exp(cs - logw)
    p_inv = jnp.exp(-cs)
    p_tot = p[0:1, :] if reverse else p[c - 1:c, :]
    rt = r * p
    at = -(kk * p_prev)
    bt = b * p_inv
    kt = kd * p_inv

    ar = jnp.concatenate([at, rt], axis=0)
    sb = _dot_nt(ar, _bd(bt, bdmask))
    sk = _dot_nt(ar, _bd(kt, bdmask))
    yield
    l_ab = jnp.where(strict, sb[0:c], 0.0)
    m_rb = jnp.where(incl, sb[c:2 * c], 0.0)
    m_ak = jnp.where(strict, sk[0:c], 0.0)
    m_rk = jnp.where(incl, sk[c:2 * c], 0.0)

    wv = _dot(jnp.concatenate([m_ak, m_rk], axis=0), _bd(v, bdmask))

    pw = l_ab
    out = _dot(jnp.concatenate([pw, m_rb], axis=0), _bd(pw, bdmask))
    yield
    t_off = l_ab
    mt = m_rb + out[c:2 * c]
    pw = out[0:c]
    n_steps = int(math.log2(c))
    for step in range(1, n_steps):
        lhs = [t_off, mt] if step == n_steps - 1 else [t_off, mt, pw]
        out = _dot(jnp.concatenate(lhs, axis=0), _bd(pw, bdmask))
        yield
        t_off = t_off + pw + out[0:c]
        mt = mt + out[c:2 * c]
        pw = out[2 * c:3 * c] if step < n_steps - 1 else None

    tx = _dot(jnp.concatenate([t_off, mt], axis=0),
              jnp.concatenate([_bd(at, bdmask), _bd(wv[0:c], bdmask)], axis=1))
    yield
    a_hat, w2 = at + tx[0:c, 0:GROUP_W], wv[0:c] + tx[0:c, GROUP_W:2 * GROUP_W]
    r_hat = rt + tx[c:2 * c, 0:GROUP_W]
    y0 = tx[c:2 * c, GROUP_W:2 * GROUP_W] + wv[c:2 * c]
    ar_hat = jnp.concatenate([a_hat, r_hat], axis=0).astype(BF16)
    wy = jnp.concatenate([w2, y0], axis=0)
    bk_t = jnp.concatenate([bt * p_tot, kt * p_tot], axis=0).T.astype(BF16)
    p_col = jnp.broadcast_to(p_tot, (2 * c, GROUP_W)).T
    p_col = jnp.concatenate([p_col, p_col], axis=1)
    return ar_hat, wy, bk_t, p_col, v


def _state_chain(st, fetches, y_ref, row_slices, lanes, bdmask):
    for fetch, rows in zip(fetches, row_slices):
        while fetch() is None:
            yield
        ar_hat, wy, bk_t, p_col, v = fetch()
        c = v.shape[0]
        uy = _dot(ar_hat, st) + wy
        yield
        u, y = uy[0:c], uy[c:2 * c]
        y_ref[rows, lanes] = y
        uv = jnp.concatenate([u, v.astype(F32)], axis=0)
        st = jnp.where(bdmask, st * p_col + _dot(bk_t, uv), 0.0)
        yield
    return st


def _scan_kernel(*refs, has_s0, emit_state):
    refs = list(refs)
    s0_ref = refs.pop(0) if has_s0 else None
    st_s = refs.pop()
    st_o = refs.pop() if emit_state else None
    (rf, vf, kf, lwf, kdf, bf, rb, vb, kb, lwb, kdb, bb, yf_o, yb_o) = refs
    i = pl.program_id(1)
    c = CHUNK
    n_chunks = rf.shape[0] // c
    n_groups = rf.shape[1] // GROUP_W

    row = lax.broadcasted_iota(jnp.int32, (GROUP_W, GROUP_W), 0)
    colm = lax.broadcasted_iota(jnp.int32, (GROUP_W, GROUP_W), 1)
    bdmask = (row // HEAD) == (colm // HEAD)
    t_row = lax.broadcasted_iota(jnp.int32, (c, GROUP_W), 0)
    t_col = lax.broadcasted_iota(jnp.int32, (c, GROUP_W), 1) % c
    tr = lax.broadcasted_iota(jnp.int32, (c, 3 * c), 0)
    tc = lax.broadcasted_iota(jnp.int32, (c, 3 * c), 1) % c
    masks_f = ((tc <= tr).astype(BF16), t_col < t_row, t_col <= t_row, bdmask)
    masks_b = ((tc >= tr).astype(BF16), t_col > t_row, t_col >= t_row, bdmask)

    @pl.when(i == 0)
    def _():
        if has_s0:
            for g in range(n_groups):
                for d in range(2):
                    st_s[g, d] = jnp.where(bdmask, s0_ref[0, 0, d, g], 0.0).T
        else:
            st_s[...] = jnp.zeros(st_s.shape, F32)

    rows = [pl.ds(j * c, c) for j in range(n_chunks)]
    lanes = [pl.ds(g * GROUP_W, GROUP_W) for g in range(n_groups)]
    fwd_refs = (rf, vf, kf, lwf, kdf, bf)
    bwd_refs = (rb, vb, kb, lwb, kdb, bb)
    tasks, starts = [], []
    for l in lanes:
        for j, s in enumerate(rows):
            tasks.append(_chunk_operands(False, *(ref[s, l] for ref in fwd_refs), masks_f))
            starts.append(j * SCAN_SKEW)
    for l in lanes:
        for j, s in enumerate(rows):
            tasks.append(_chunk_operands(True, *(ref[s, l] for ref in bwd_refs), masks_b))
            starts.append((n_chunks - 1 - j) * SCAN_SKEW)
    n_ops = len(tasks)
    results = [None] * (n_ops + 2 * n_groups)
    fetch = lambda k: (lambda: results[k])
    for g in range(n_groups):
        f_ids = [g * n_chunks + j for j in range(n_chunks)]
        b_ids = [(n_groups + g) * n_chunks + j for j in range(n_chunks)]
        tasks.append(_state_chain(st_s[g, 0], [fetch(k) for k in f_ids], yf_o, rows, lanes[g],
                                  bdmask))
        tasks.append(_state_chain(st_s[g, 1], [fetch(k) for k in b_ids[::-1]], yb_o, rows[::-1],
                                  lanes[g], bdmask))
        starts += [0, 0]
    _interleave(tasks, starts, results)
    for g in range(n_groups):
        for d in range(2):
            st_s[g, d] = results[n_ops + 2 * g + d]

    if emit_state:
        @pl.when(i == pl.num_programs(1) - 1)
        def _():
            for g in range(n_groups):
                for d in range(2):
                    st_o[0, d, g] = results[n_ops + 2 * g + d].T


def _scan(prep, s0, layer, seq, emit_state):
    r, v, kn, lw0, kd0, b0, lw1, kd1, b1 = prep
    n, wc = r.shape
    nb = n // seq
    ng = wc // GROUP_W
    blk = min(SCAN_BLOCK, seq)
    nt = seq // blk
    fwd = pl.BlockSpec((blk, wc), lambda b, i: (b * nt + i, 0))
    bwd = pl.BlockSpec((blk, wc), lambda b, i: (b * nt + nt - 1 - i, 0))
    ins = [r, v, kn, lw0, kd0, b0, r, v, kn, lw1, kd1, b1]
    in_specs = [fwd] * 6 + [bwd] * 6
    if s0 is not None:
        ins = [s0] + ins
        in_specs = [pl.BlockSpec((1, 1, 2, ng, GROUP_W, GROUP_W),
                                 lambda b, i: (b, layer, 0, 0, 0, 0))] + in_specs
    y_shape = jax.ShapeDtypeStruct((n, wc), F32)
    out_specs, out_shape = [fwd, bwd], [y_shape, y_shape]
    if emit_state:
        out_specs.append(pl.BlockSpec((1, 2, ng, GROUP_W, GROUP_W), lambda b, i: (b, 0, 0, 0, 0)))
        out_shape.append(jax.ShapeDtypeStruct((nb, 2, ng, GROUP_W, GROUP_W), F32))
    outs = pl.pallas_call(
        functools.partial(_scan_kernel, has_s0=s0 is not None, emit_state=emit_state),
        grid=(nb, nt),
        in_specs=in_specs,
        out_specs=out_specs,
        out_shape=out_shape,
        scratch_shapes=[pltpu.VMEM((ng, 2, GROUP_W, GROUP_W), F32)],
        compiler_params=_cparams(("parallel", "arbitrary")),
        name="rwkv_scan",
    )(*ins)
    return (outs[0], outs[1], outs[2] if emit_state else None)


def _outproj_kernel(x_ref, ada_ref, ya_ref, u_ref, sgb_ref, yf_ref, ybw_ref, bonus_ref, sgc_ref,
                    cw_ref, cb_ref, lg_ref, lb_ref, gng_ref, gnb_ref, w_ref, gp_ref, ones_ref,
                    o_ref, yb_ref, *pad_refs, d, seq, tiles_per_seq):
    conv_args = (u_ref, sgb_ref, cw_ref, cb_ref, lg_ref, lb_ref, yb_ref)
    if tiles_per_seq is None:
        _convb_sequences(*conv_args, pad_refs[0], seq=seq, pad_rows=(pad_refs[0].shape[1] - seq) // 2,
                         row_block=64)
    else:
        n_out = yb_ref.shape[0] // GRID_W
        it = pl.program_id(0) % tiles_per_seq
        for t in range(tiles_per_seq):
            pl.when(it == t)(functools.partial(_convb_grid_rows, *conv_args, first_row=t * n_out,
                                               n_out=n_out, n_rows=seq // GRID_W))

    ones_g = ones_ref[...]
    ys = yf_ref[...] + ybw_ref[...]
    mu = _head_sum(ys, ones_g) * (1.0 / HEAD)
    dev = ys - mu
    var = _head_sum(dev * dev, ones_g) * (1.0 / HEAD)
    yg = dev * lax.rsqrt(var + GN_EPS) * gng_ref[0] + gnb_ref[0]
    yc = (yg + bonus_ref[...]) * sgc_ref[...]
    cat = jnp.concatenate([ya_ref[...], yb_ref[...], yc.astype(BF16)], axis=-1)
    out = jnp.dot(cat, w_ref[0], preferred_element_type=F32)
    gate = ada_ref[0, 0][:, 2 * d:3 * d]
    nrm = out * lax.rsqrt(jnp.mean(out * out, -1, keepdims=True) + RMS_EPS) * gp_ref[0]
    o_ref[...] = x_ref[...] + gate * nrm


def _outproj(x, ada, ada_row, ya, ub, sgb, yf, ybw, bonus, sgc, p, layer, seq, latent):
    n, d = x.shape
    wc = yf.shape[1]
    wa = ya.shape[1]
    wb = ub.shape[1]
    tile, per_b = _token_tile(OUT_TILE, n, seq, ada_row[1])
    tok = lambda w: pl.BlockSpec((tile, w), lambda i: (i, 0))
    if latent:
        assert seq % tile == 0 and tile % GRID_W == 0
        tiles_per_seq = seq // tile
        u_spec = pl.BlockSpec((seq, wb), lambda i: (i // tiles_per_seq, 0))
        pads = []
    else:
        assert tile % seq == 0
        tiles_per_seq = None
        u_spec = tok(wb)
        pad_rows = -(-CONV_B_HALF // SUBLANES) * SUBLANES
        pads = [pltpu.VMEM((tile // seq, seq + 2 * pad_rows, wb), F32)]
    params = [p[k] for k in ('conv_b', 'conv_b_bias', 'ln_b_g', 'ln_b_b', 'gn_g', 'gn_b', 'w_out',
                             'g_post')]
    return pl.pallas_call(
        functools.partial(_outproj_kernel, d=d, seq=seq, tiles_per_seq=tiles_per_seq),
        grid=(n // tile,),
        in_specs=[tok(d), _ada_spec(ada, layer, ada_row[0], per_b), tok(wa), u_spec, tok(wb), tok(wc),
                  tok(wc), tok(wc), tok(wc)] + [_layer_spec(a, layer) for a in params]
                 + [pl.BlockSpec(p['ones_g'].shape, lambda i: (0, 0))],
        out_specs=tok(d),
        out_shape=jax.ShapeDtypeStruct((n, d), F32),
        scratch_shapes=[pltpu.VMEM((tile, wb), BF16)] + pads,
        compiler_params=_cparams(("parallel",)),
        name="outproj",
    )(x, ada, ya, ub, sgb, yf, ybw, bonus, sgc, *params, p['ones_g'])


def _trunk_layer(x, ada, ada_row, p, layer, s0, seq, latent):
    ya, ub, sgb, sgc, *ops = _inproj(x, ada, ada_row, p, layer, seq, GRID_W if latent else seq)
    yf, ybw, st = _scan(ops[:9], s0, layer, seq, emit_state=not latent)
    x = _outproj(x, ada, ada_row, ya, ub, sgb, yf, ybw, ops[9], sgc, p, layer, seq, latent)
    return x, st


def _unpack_states(st):
    b, nd, g = st.shape[:3]
    diag = jnp.stack([st[:, :, :, h * HEAD:(h + 1) * HEAD, h * HEAD:(h + 1) * HEAD]
                      for h in range(HEADS_PER_GROUP)], axis=3)
    return diag.reshape(b, nd, g * HEADS_PER_GROUP, HEAD, HEAD)


def kernel(x_prompt, x_sample, c, state_rwkv, c_ctx, ada_w, ada_b, g_pre, g_post, w_in, conv_a,
           conv_b, conv_b_bias, ln_b_g, ln_b_b, mu, w0, w2, a0, a2, k_k, k_a, r_k, gn_g, gn_b, w_out):
    bp, seq_p, d = x_prompt.shape
    bs, seq_s, _ = x_sample.shape
    depth = ada_w.shape[0]
    wc = k_k.shape[1]
    n_heads = wc // HEAD
    n_groups = n_heads // HEADS_PER_GROUP

    rows = -(-(1 + bs) // SUBLANES) * SUBLANES
    mods = jnp.zeros((rows, d), F32).at[0].set(c_ctx).at[1:1 + bs].set(c)
    ada = _ada(mods, ada_w, ada_b).reshape(depth, rows, 1, 3 * d)
    head_id = jnp.arange(GROUP_W) // HEAD
    row3 = lambda a: a.reshape(depth, 1, -1)
    p = {'g_pre': row3(g_pre), 'g_post': row3(g_post), 'w_in': w_in.astype(BF16), 'conv_a': conv_a,
         'conv_b': conv_b, 'conv_b_bias': row3(conv_b_bias), 'ln_b_g': row3(ln_b_g),
         'ln_b_b': row3(ln_b_b), 'mu': row3(mu), 'w0': w0, 'a0': a0,
         'w2': w2.reshape(depth, 2 * LORA, wc).astype(BF16),
         'a2': a2.reshape(depth, 2 * LORA, wc).astype(BF16),
         'k_k': row3(k_k), 'k_a': row3(k_a), 'r_k': row3(r_k), 'gn_g': row3(gn_g),
         'gn_b': row3(gn_b), 'w_out': w_out.astype(BF16),
         'ones_g': (head_id[:, None] == head_id[None, :]).astype(BF16)}
    s0 = state_rwkv.reshape(bs, depth, 2, n_groups, GROUP_W, HEAD)
    s0 = jnp.tile(s0, (1, 1, 1, 1, 1, HEADS_PER_GROUP))

    ctx_row = (0, False)
    lat_row = (1, True)
    xp = x_prompt.reshape(bp * seq_p, d)
    xs = x_sample.reshape(bs * seq_s, d)
    new_states = []
    for l in range(depth):
        xp, st = _trunk_layer(xp, ada, ctx_row, p, l, None, seq_p, False)
        new_states.append(_unpack_states(st))
        xs, _ = _trunk_layer(xs, ada, lat_row, p, l, s0, seq_s, True)
    return (xp.reshape(bp, seq_p, d), xs.reshape(bs, seq_s, d), jnp.stack(new_states, axis=1))
```

```python
import functools
import math

import jax
import jax.numpy as jnp
from jax import lax
from jax.experimental import pallas as pl
from jax.experimental.pallas import tpu as pltpu

F32 = jnp.float32
BF16 = jnp.bfloat16

SUBLANES = 8
LANES = 128
GRID_W = 64
HEAD = 64
HEADS_PER_GROUP = 4
GROUP_W = HEAD * HEADS_PER_GROUP
CHUNK = 64
CONV_B_HALF = 15
LORA = 64
RMS_EPS = 1e-6
LN_EPS = 1e-5
GN_EPS = 64e-5
DECAY_SCALE = math.exp(-0.5)

TOK_TILE = 512
OUT_TILE = 1024
SCAN_BLOCK = 512
SCAN_SKEW = 2
VMEM_LIMIT = 56 * 1024 * 1024


def _cparams(sem):
    return pltpu.CompilerParams(dimension_semantics=sem, vmem_limit_bytes=VMEM_LIMIT)


def _layer_spec(a, layer):
    return pl.BlockSpec((1,) + a.shape[1:], lambda *_: (layer,) + (0,) * (a.ndim - 1))


def _dot(a, b):
    return jnp.dot(a.astype(BF16), b.astype(BF16), preferred_element_type=F32)


def _dot_nt(a, b):
    return lax.dot_general(a.astype(BF16), b.astype(BF16), (((1,), (1,)), ((), ())),
                           preferred_element_type=F32)


def _split3(x):
    h1 = x.astype(BF16)
    r1 = x - h1.astype(F32)
    h2 = r1.astype(BF16)
    h3 = (r1 - h2.astype(F32)).astype(BF16)
    return h1, h2, h3


def _head_sum(x, ones_g):
    parts = [_dot(x[:, g * GROUP_W:(g + 1) * GROUP_W], ones_g) for g in range(x.shape[1] // GROUP_W)]
    return jnp.concatenate(parts, axis=1)


def _silu(x):
    return x * jax.nn.sigmoid(x)


def _interleave(tasks, starts, results):
    live = list(range(len(tasks)))
    rnd = 0
    while live:
        for k in list(live):
            if rnd >= starts[k]:
                try:
                    next(tasks[k])
                except StopIteration as done:
                    results[k] = done.value
                    live.remove(k)
        rnd += 1


def _split2(x):
    hi = x.astype(BF16)
    return hi, (x - hi.astype(F32)).astype(BF16)


def _ada_kernel(mod_ref, w_ref, b_ref, o_ref):
    a_hi, a_lo = _split2(_silu(mod_ref[...]))
    w_hi, w_lo = _split2(w_ref[0])
    rows = a_hi.shape[0]
    top = jnp.dot(jnp.concatenate([a_hi, a_lo], axis=0), w_hi, preferred_element_type=F32)
    o_ref[0] = (top[0:rows] + top[rows:2 * rows]
                + jnp.dot(a_hi, w_lo, preferred_element_type=F32) + b_ref[0])


def _ada(mods, ada_w, ada_b):
    n_layers, d, d3 = ada_w.shape
    rows = mods.shape[0]
    blk = d
    return pl.pallas_call(
        _ada_kernel,
        grid=(n_layers, d3 // blk),
        in_specs=[pl.BlockSpec((rows, d), lambda l, j: (0, 0)),
                  pl.BlockSpec((1, d, blk), lambda l, j: (l, 0, j)),
                  pl.BlockSpec((1, 1, blk), lambda l, j: (l, 0, j))],
        out_specs=pl.BlockSpec((1, rows, blk), lambda l, j: (l, 0, j)),
        out_shape=jax.ShapeDtypeStruct((n_layers, rows, d3), F32),
        compiler_params=_cparams(("parallel", "parallel")),
        name="ada",
    )(mods, ada_w, ada_b.reshape(n_layers, 1, d3))


def _ada_spec(ada, layer, first_row, per_b):
    if per_b is None:
        index = lambda i: (layer, first_row, 0, 0)
    else:
        index = lambda i: (layer, first_row + i // per_b, 0, 0)
    return pl.BlockSpec((1, 1, 1, ada.shape[3]), index)


def _rwkv_operands(zc, prev_row, next_row, first_pos, seq, mu_ref, w0_ref, w2_ref, a0_ref, a2_ref,
                   kk_ref, ka_ref, rk_ref, ones_ref, outs, wc):
    r_o, v_o, kn_o, lw0_o, kd0_o, b0_o, lw1_o, kd1_o, b1_o, bonus_o = outs
    rows = zc.shape[0]
    ridx = lax.broadcasted_iota(jnp.int32, zc.shape, 0)
    up = jnp.where(ridx == 0, prev_row, pltpu.roll(zc, 1, 0))
    dn = jnp.where(ridx == rows - 1, next_row, pltpu.roll(zc, rows - 1, 0))
    spos = (first_pos + lax.broadcasted_iota(jnp.int32, (rows, LANES), 0)) % seq
    across = lambda m: jnp.concatenate([m.astype(F32)] * (zc.shape[1] // LANES), axis=1)
    up = up * across(spos != 0)
    dn = dn * across(spos != seq - 1)
    zs = zc + mu_ref[0] * (0.5 * (up + dn) - zc)

    r, k, v = zs[:, 0:wc], zs[:, wc:2 * wc], zs[:, 2 * wc:3 * wc]
    lw = jnp.tanh(zs[:, 3 * wc:3 * wc + 2 * LORA])
    la = zs[:, 3 * wc + 2 * LORA:3 * wc + 4 * LORA]
    lane = lax.broadcasted_iota(jnp.int32, lw.shape, 1)
    ones_g = ones_ref[...]

    r_o[...] = r
    v_o[...] = v.astype(v_o.dtype)
    kk = k * kk_ref[0]
    kk_ss = _head_sum(kk * kk, ones_g)
    yield
    kn = kk * lax.rsqrt(kk_ss + 1e-12)
    kn_o[...] = kn

    kd_sum = jnp.zeros_like(k)
    for d, (lw_o, kd_o, b_o) in enumerate(((lw0_o, kd0_o, b0_o), (lw1_o, kd1_o, b1_o))):
        sel = (lane >= d * LORA) & (lane < (d + 1) * LORA)
        wl = w0_ref[0, d:d + 1, :] + _dot(jnp.where(sel, lw, 0.0), w2_ref[0])
        a = jax.nn.sigmoid(a0_ref[0, d:d + 1, :] + _dot(jnp.where(sel, la, 0.0), a2_ref[0]))
        yield
        kd = k * (1.0 + (a - 1.0) * ka_ref[0])
        lw_o[...] = -DECAY_SCALE * jax.nn.sigmoid(wl)
        kd_o[...] = kd
        b_o[...] = kn * a
        kd_sum = kd_sum + kd
    bonus_o[...] = _head_sum(r * kd_sum * rk_ref[0], ones_g) * v


def _inproj_kernel(x_ref, xp_ref, xn_ref, ada_ref, g_ref, w_ref, ca_ref, mu_ref, w0_ref, w2_ref,
                   a0_ref, a2_ref, kk_ref, ka_ref, rk_ref, ones_ref,
                   ya_ref, ub_ref, sgb_ref, sgc_ref, *c_outs, d, wa, wb, shift_w, seg, seq):
    ada = ada_ref[0, 0]
    shift, scale = ada[:, 0:d], ada[:, d:2 * d]

    def mod_norm(x):
        y = x * lax.rsqrt(jnp.mean(x * x, -1, keepdims=True) + RMS_EPS) * g_ref[0]
        return y * (1.0 + scale) + shift

    h32 = mod_norm(x_ref[...])
    h = h32.astype(BF16)
    rows = h.shape[0]
    proj = lambda lo, hi: jnp.dot(h, w_ref[0, :, lo:hi], preferred_element_type=F32)
    off_b = 4 * wa
    off_c = off_b + 3 * wb

    h_ext = jnp.concatenate([mod_norm(xp_ref[...]), h32, mod_norm(xn_ref[...])], axis=0)
    zc_ext = jnp.dot(h_ext.astype(BF16), w_ref[0, :, off_c:off_c + shift_w],
                     preferred_element_type=F32)
    prev_row = zc_ext[SUBLANES - 1:SUBLANES, :]
    next_row = zc_ext[SUBLANES + rows:SUBLANES + rows + 1, :]

    def branches():
        za = proj(0, off_b)
        yield
        xa, bg, cg, ga = (za[:, k * wa:(k + 1) * wa] for k in range(4))
        u = cg * xa
        pos = lax.broadcasted_iota(jnp.int32, u.shape, 0) % seg
        prev = jnp.where(pos == 0, 0.0, pltpu.roll(u, 1, 0))
        nxt = jnp.where(pos == seg - 1, 0.0, pltpu.roll(u, rows - 1, 0))
        conv = ca_ref[0, 0:1, :] * prev + ca_ref[0, 1:2, :] * u + ca_ref[0, 2:3, :] * nxt
        ya_ref[...] = (bg * conv * _silu(ga)).astype(ya_ref.dtype)
        zb = proj(off_b, off_c)
        yield
        ub_ref[...] = zb[:, 0:wb] * jax.nn.sigmoid(zb[:, wb:2 * wb])
        sgb_ref[...] = _silu(zb[:, 2 * wb:3 * wb])
        sgc_ref[...] = _silu(proj(off_c + shift_w, w_ref.shape[2]))

    tasks = [branches(),
             _rwkv_operands(zc_ext[SUBLANES:SUBLANES + rows, :], prev_row, next_row,
                            pl.program_id(0) * rows, seq, mu_ref, w0_ref, w2_ref, a0_ref, a2_ref,
                            kk_ref, ka_ref, rk_ref, ones_ref, c_outs, kk_ref.shape[2])]
    _interleave(tasks, [0, 0], [None, None])


def _token_tile(limit, n, seq, ada_per_seq):
    tile = min(limit, seq if ada_per_seq else n)
    assert seq % tile == 0 or tile % seq == 0
    return tile, (seq // tile if ada_per_seq else None)


def _inproj(x, ada, ada_row, p, layer, seq, seg):
    n, d = x.shape
    d_in = p['w_in'].shape[2]
    wa = p['conv_a'].shape[2]
    wb = p['conv_b'].shape[2]
    shift_w = p['mu'].shape[2]
    wc = p['k_k'].shape[2]
    assert d_in == 4 * wa + 3 * wb + shift_w + wc
    tile, per_b = _token_tile(TOK_TILE, n, seq, ada_row[1])
    assert tile % seg == 0
    sub = tile // SUBLANES
    n_sub = n // SUBLANES
    tok = lambda w: pl.BlockSpec((tile, w), lambda i: (i, 0))
    names = ('g_pre', 'w_in', 'conv_a', 'mu', 'w0', 'w2', 'a0', 'a2', 'k_k', 'k_a', 'r_k')
    params = [p[k] for k in names]
    outs = [(wa, BF16), (wb, F32), (wb, F32), (wc, F32)] + [
        (wc, BF16 if k == 1 else F32) for k in range(10)]
    return pl.pallas_call(
        functools.partial(_inproj_kernel, d=d, wa=wa, wb=wb, shift_w=shift_w, seg=seg, seq=seq),
        grid=(n // tile,),
        in_specs=[tok(d),
                  pl.BlockSpec((SUBLANES, d), lambda i: (jnp.maximum(i * sub - 1, 0), 0)),
                  pl.BlockSpec((SUBLANES, d), lambda i: (jnp.minimum((i + 1) * sub, n_sub - 1), 0)),
                  _ada_spec(ada, layer, ada_row[0], per_b)]
                 + [_layer_spec(a, layer) for a in params]
                 + [pl.BlockSpec(p['ones_g'].shape, lambda i: (0, 0))],
        out_specs=[tok(w) for w, _ in outs],
        out_shape=[jax.ShapeDtypeStruct((n, w), dt) for w, dt in outs],
        compiler_params=_cparams(("parallel",)),
        name="inproj",
    )(x, x, x, ada, *params, p['ones_g'])


def _convb_finish(u, sgb, lg_ref, lb_ref):
    mu = jnp.mean(u, -1, keepdims=True)
    var = jnp.mean(jnp.square(u - mu), -1, keepdims=True)
    ln = (u - mu) * lax.rsqrt(var + LN_EPS) * lg_ref[0] + lb_ref[0]
    return _silu(ln) * sgb


def _convb_grid_rows(u_ref, sgb_ref, w_ref, bias_ref, lg_ref, lb_ref, yb_ref, *, first_row, n_out,
                     n_rows):
    for k in range(n_out):
        r = first_row + k
        accs = [None, None]
        for m, q in enumerate(range(max(0, r - CONV_B_HALF), min(n_rows - 1, r + CONV_B_HALF) + 1)):
            j = q - r + CONV_B_HALF
            term = w_ref[0, j:j + 1, :] * u_ref[q * GRID_W:(q + 1) * GRID_W, :]
            accs[m % 2] = term if accs[m % 2] is None else accs[m % 2] + term
        u = accs[0] + accs[1] + bias_ref[0]
        out = slice(k * GRID_W, (k + 1) * GRID_W)
        yb_ref[out, :] = _convb_finish(u, sgb_ref[out, :], lg_ref, lb_ref).astype(yb_ref.dtype)


def _convb_sequences(u_ref, sgb_ref, w_ref, bias_ref, lg_ref, lb_ref, yb_ref, pad_ref, *, seq,
                     pad_rows, row_block):
    zeros = jnp.zeros((pad_rows, pad_ref.shape[2]), F32)
    first = pad_rows - CONV_B_HALF
    for s in range(pad_ref.shape[0]):
        pad_ref[s, 0:pad_rows, :] = zeros
        pad_ref[s, pad_rows + seq:pad_rows + seq + pad_rows, :] = zeros
        pad_ref[s, pad_rows:pad_rows + seq, :] = u_ref[s * seq:(s + 1) * seq, :]
        for i in range(seq // row_block):
            start = i * row_block
            accs = [None, None]
            for j in range(2 * CONV_B_HALF + 1):
                lo = start + first + j
                term = w_ref[0, j:j + 1, :] * pad_ref[s, lo:lo + row_block, :]
                accs[j % 2] = term if accs[j % 2] is None else accs[j % 2] + term
            u = accs[0] + accs[1] + bias_ref[0]
            out = slice(s * seq + start, s * seq + start + row_block)
            yb_ref[out, :] = _convb_finish(u, sgb_ref[out, :], lg_ref, lb_ref).astype(yb_ref.dtype)


def _bd(y, bdmask):
    return jnp.where(bdmask, jnp.concatenate([y] * HEADS_PER_GROUP, axis=0), 0.0).astype(BF16)


def _chunk_operands(reverse, r, v, kk, logw, kd, b, masks):
    tri3, strict, incl, bdmask = masks
    c = r.shape[0]
    cs = jnp.dot(tri3, jnp.concatenate(_split3(logw), axis=0), preferred_element_type=F32)
    yield
    p = jnp.exp(cs)
    p_prev = jnp.exp(cs - logw)
    p_inv = jnp.exp(-cs)
    p_tot = p[0:1, :] if reverse else p[c - 1:c, :]
    rt = r * p
    at = -(kk * p_prev)
    bt = b * p_inv
    kt = kd * p_inv

    ar = jnp.concatenate([at, rt], axis=0)
    sb = _dot_nt(ar, _bd(bt, bdmask))
    sk = _dot_nt(ar, _bd(kt, bdmask))
    yield
    l_ab = jnp.where(strict, sb[0:c], 0.0)
    m_rb = jnp.where(incl, sb[c:2 * c], 0.0)
    m_ak = jnp.where(strict, sk[0:c], 0.0)
    m_rk = jnp.where(incl, sk[c:2 * c], 0.0)

    wv = _dot(jnp.concatenate([m_ak, m_rk], axis=0), _bd(v, bdmask))

    pw = l_ab
    out = _dot(jnp.concatenate([pw, m_rb], axis=0), _bd(pw, bdmask))
    yield
    t_off = l_ab
    mt = m_rb + out[c:2 * c]
    pw = out[0:c]
    n_steps = int(math.log2(c))
    for step in range(1, n_steps):
        lhs = [t_off, mt] if step == n_steps - 1 else [t_off, mt, pw]
        out = _dot(jnp.concatenate(lhs, axis=0), _bd(pw, bdmask))
        yield
        t_off = t_off + pw + out[0:c]
        mt = mt + out[c:2 * c]
        pw = out[2 * c:3 * c] if step < n_steps - 1 else None

    tx = _dot(jnp.concatenate([t_off, mt], axis=0),
              jnp.concatenate([_bd(at, bdmask), _bd(wv[0:c], bdmask)], axis=1))
    yield
    a_hat, w2 = at + tx[0:c, 0:GROUP_W], wv[0:c] + tx[0:c, GROUP_W:2 * GROUP_W]
    r_hat = rt + tx[c:2 * c, 0:GROUP_W]
    y0 = tx[c:2 * c, GROUP_W:2 * GROUP_W] + wv[c:2 * c]
    ar_hat = jnp.concatenate([a_hat, r_hat], axis=0).astype(BF16)
    wy = jnp.concatenate([w2, y0], axis=0)
    bk_t = jnp.concatenate([bt * p_tot, kt * p_tot], axis=0).T.astype(BF16)
    p_col = jnp.broadcast_to(p_tot, (2 * c, GROUP_W)).T
    p_col = jnp.concatenate([p_col, p_col], axis=1)
    return ar_hat, wy, bk_t, p_col, v


def _state_chain(st, fetches, y_ref, row_slices, lanes, bdmask):
    for fetch, rows in zip(fetches, row_slices):
        while fetch() is None:
            yield
        ar_hat, wy, bk_t, p_col, v = fetch()
        c = v.shape[0]
        uy = _dot(ar_hat, st) + wy
        yield
        u, y = uy[0:c], uy[c:2 * c]
        y_ref[rows, lanes] = y
        uv = jnp.concatenate([u, v.astype(F32)], axis=0)
        st = jnp.where(bdmask, st * p_col + _dot(bk_t, uv), 0.0)
        yield
    return st


def _scan_kernel(*refs, has_s0, emit_state):
    refs = list(refs)
    s0_ref = refs.pop(0) if has_s0 else None
    st_s = refs.pop()
    st_o = refs.pop() if emit_state else None
    (rf, vf, kf, lwf, kdf, bf, rb, vb, kb, lwb, kdb, bb, yf_o, yb_o) = refs
    i = pl.program_id(1)
    c = CHUNK
    n_chunks = rf.shape[0] // c
    n_groups = rf.shape[1] // GROUP_W

    row = lax.broadcasted_iota(jnp.int32, (GROUP_W, GROUP_W), 0)
    colm = lax.broadcasted_iota(jnp.int32, (GROUP_W, GROUP_W), 1)
    bdmask = (row // HEAD) == (colm // HEAD)
    t_row = lax.broadcasted_iota(jnp.int32, (c, GROUP_W), 0)
    t_col = lax.broadcasted_iota(jnp.int32, (c, GROUP_W), 1) % c
    tr = lax.broadcasted_iota(jnp.int32, (c, 3 * c), 0)
    tc = lax.broadcasted_iota(jnp.int32, (c, 3 * c), 1) % c
    masks_f = ((tc <= tr).astype(BF16), t_col < t_row, t_col <= t_row, bdmask)
    masks_b = ((tc >= tr).astype(BF16), t_col > t_row, t_col >= t_row, bdmask)

    @pl.when(i == 0)
    def _():
        if has_s0:
            for g in range(n_groups):
                for d in range(2):
                    st_s[g, d] = jnp.where(bdmask, s0_ref[0, 0, d, g], 0.0).T
        else:
            st_s[...] = jnp.zeros(st_s.shape, F32)

    rows = [pl.ds(j * c, c) for j in range(n_chunks)]
    lanes = [pl.ds(g * GROUP_W, GROUP_W) for g in range(n_groups)]
    fwd_refs = (rf, vf, kf, lwf, kdf, bf)
    bwd_refs = (rb, vb, kb, lwb, kdb, bb)
    tasks, starts = [], []
    for l in lanes:
        for j, s in enumerate(rows):
            tasks.append(_chunk_operands(False, *(ref[s, l] for ref in fwd_refs), masks_f))
            starts.append(j * SCAN_SKEW)
    for l in lanes:
        for j, s in enumerate(rows):
            tasks.append(_chunk_operands(True, *(ref[s, l] for ref in bwd_refs), masks_b))
            starts.append((n_chunks - 1 - j) * SCAN_SKEW)
    n_ops = len(tasks)
    results = [None] * (n_ops + 2 * n_groups)
    fetch = lambda k: (lambda: results[k])
    for g in range(n_groups):
        f_ids = [g * n_chunks + j for j in range(n_chunks)]
        b_ids = [(n_groups + g) * n_chunks + j for j in range(n_chunks)]
        tasks.append(_state_chain(st_s[g, 0], [fetch(k) for k in f_ids], yf_o, rows, lanes[g],
                                  bdmask))
        tasks.append(_state_chain(st_s[g, 1], [fetch(k) for k in b_ids[::-1]], yb_o, rows[::-1],
                                  lanes[g], bdmask))
        starts += [0, 0]
    _interleave(tasks, starts, results)
    for g in range(n_groups):
        for d in range(2):
            st_s[g, d] = results[n_ops + 2 * g + d]

    if emit_state:
        @pl.when(i == pl.num_programs(1) - 1)
        def _():
            for g in range(n_groups):
                for d in range(2):
                    st_o[0, d, g] = results[n_ops + 2 * g + d].T


def _scan(prep, s0, layer, seq, emit_state):
    r, v, kn, lw0, kd0, b0, lw1, kd1, b1 = prep
    n, wc = r.shape
    nb = n // seq
    ng = wc // GROUP_W
    blk = min(SCAN_BLOCK, seq)
    nt = seq // blk
    fwd = pl.BlockSpec((blk, wc), lambda b, i: (b * nt + i, 0))
    bwd = pl.BlockSpec((blk, wc), lambda b, i: (b * nt + nt - 1 - i, 0))
    ins = [r, v, kn, lw0, kd0, b0, r, v, kn, lw1, kd1, b1]
    in_specs = [fwd] * 6 + [bwd] * 6
    if s0 is not None:
        ins = [s0] + ins
        in_specs = [pl.BlockSpec((1, 1, 2, ng, GROUP_W, GROUP_W),
                                 lambda b, i: (b, layer, 0, 0, 0, 0))] + in_specs
    y_shape = jax.ShapeDtypeStruct((n, wc), F32)
    out_specs, out_shape = [fwd, bwd], [y_shape, y_shape]
    if emit_state:
        out_specs.append(pl.BlockSpec((1, 2, ng, GROUP_W, GROUP_W), lambda b, i: (b, 0, 0, 0, 0)))
        out_shape.append(jax.ShapeDtypeStruct((nb, 2, ng, GROUP_W, GROUP_W), F32))
    outs = pl.pallas_call(
        functools.partial(_scan_kernel, has_s0=s0 is not None, emit_state=emit_state),
        grid=(nb, nt),
        in_specs=in_specs,
        out_specs=out_specs,
        out_shape=out_shape,
        scratch_shapes=[pltpu.VMEM((ng, 2, GROUP_W, GROUP_W), F32)],
        compiler_params=_cparams(("parallel", "arbitrary")),
        name="rwkv_scan",
    )(*ins)
    return (outs[0], outs[1], outs[2] if emit_state else None)


def _outproj_kernel(x_ref, ada_ref, ya_ref, u_ref, sgb_ref, yf_ref, ybw_ref, bonus_ref, sgc_ref,
                    cw_ref, cb_ref, lg_ref, lb_ref, gng_ref, gnb_ref, w_ref, gp_ref, ones_ref,
                    o_ref, yb_ref, *pad_refs, d, seq, tiles_per_seq):
    conv_args = (u_ref, sgb_ref, cw_ref, cb_ref, lg_ref, lb_ref, yb_ref)
    if tiles_per_seq is None:
        _convb_sequences(*conv_args, pad_refs[0], seq=seq, pad_rows=(pad_refs[0].shape[1] - seq) // 2,
                         row_block=64)
    else:
        n_out = yb_ref.shape[0] // GRID_W
        it = pl.program_id(0) % tiles_per_seq
        for t in range(tiles_per_seq):
            pl.when(it == t)(functools.partial(_convb_grid_rows, *conv_args, first_row=t * n_out,
                                               n_out=n_out, n_rows=seq // GRID_W))

    ones_g = ones_ref[...]
    ys = yf_ref[...] + ybw_ref[...]
    mu = _head_sum(ys, ones_g) * (1.0 / HEAD)
    dev = ys - mu
    var = _head_sum(dev * dev, ones_g) * (1.0 / HEAD)
    yg = dev * lax.rsqrt(var + GN_EPS) * gng_ref[0] + gnb_ref[0]
    yc = (yg + bonus_ref[...]) * sgc_ref[...]
    cat = jnp.concatenate([ya_ref[...], yb_ref[...], yc.astype(BF16)], axis=-1)
    out = jnp.dot(cat, w_ref[0], preferred_element_type=F32)
    gate = ada_ref[0, 0][:, 2 * d:3 * d]
    nrm = out * lax.rsqrt(jnp.mean(out * out, -1, keepdims=True) + RMS_EPS) * gp_ref[0]
    o_ref[...] = x_ref[...] + gate * nrm


def _outproj(x, ada, ada_row, ya, ub, sgb, yf, ybw, bonus, sgc, p, layer, seq, latent):
    n, d = x.shape
    wc = yf.shape[1]
    wa = ya.shape[1]
    wb = ub.shape[1]
    tile, per_b = _token_tile(OUT_TILE, n, seq, ada_row[1])
    tok = lambda w: pl.BlockSpec((tile, w), lambda i: (i, 0))
    if latent:
        assert seq % tile == 0 and tile % GRID_W == 0
        tiles_per_seq = seq // tile
        u_spec = pl.BlockSpec((seq, wb), lambda i: (i // tiles_per_seq, 0))
        pads = []
    else:
        assert tile % seq == 0
        tiles_per_seq = None
        u_spec = tok(wb)
        pad_rows = -(-CONV_B_HALF // SUBLANES) * SUBLANES
        pads = [pltpu.VMEM((tile // seq, seq + 2 * pad_rows, wb), F32)]
    params = [p[k] for k in ('conv_b', 'conv_b_bias', 'ln_b_g', 'ln_b_b', 'gn_g', 'gn_b', 'w_out',
                             'g_post')]
    return pl.pallas_call(
        functools.partial(_outproj_kernel, d=d, seq=seq, tiles_per_seq=tiles_per_seq),
        grid=(n // tile,),
        in_specs=[tok(d), _ada_spec(ada, layer, ada_row[0], per_b), tok(wa), u_spec, tok(wb), tok(wc),
                  tok(wc), tok(wc), tok(wc)] + [_layer_spec(a, layer) for a in params]
                 + [pl.BlockSpec(p['ones_g'].shape, lambda i: (0, 0))],
        out_specs=tok(d),
        out_shape=jax.ShapeDtypeStruct((n, d), F32),
        scratch_shapes=[pltpu.VMEM((tile, wb), BF16)] + pads,
        compiler_params=_cparams(("parallel",)),
        name="outproj",
    )(x, ada, ya, ub, sgb, yf, ybw, bonus, sgc, *params, p['ones_g'])


def _trunk_layer(x, ada, ada_row, p, layer, s0, seq, latent):
    ya, ub, sgb, sgc, *ops = _inproj(x, ada, ada_row, p, layer, seq, GRID_W if latent else seq)
    yf, ybw, st = _scan(ops[:9], s0, layer, seq, emit_state=not latent)
    x = _outproj(x, ada, ada_row, ya, ub, sgb, yf, ybw, ops[9], sgc, p, layer, seq, latent)
    return x, st


def _unpack_states(st):
    b, nd, g = st.shape[:3]
    diag = jnp.stack([st[:, :, :, h * HEAD:(h + 1) * HEAD, h * HEAD:(h + 1) * HEAD]
                      for h in range(HEADS_PER_GROUP)], axis=3)
    return diag.reshape(b, nd, g * HEADS_PER_GROUP, HEAD, HEAD)


def kernel(x_prompt, x_sample, c, state_rwkv, c_ctx, ada_w, ada_b, g_pre, g_post, w_in, conv_a,
           conv_b, conv_b_bias, ln_b_g, ln_b_b, mu, w0, w2, a0, a2, k_k, k_a, r_k, gn_g, gn_b, w_out):
    bp, seq_p, d = x_prompt.shape
    bs, seq_s, _ = x_sample.shape
    depth = ada_w.shape[0]
    wc = k_k.shape[1]
    n_heads = wc // HEAD
    n_groups = n_heads // HEADS_PER_GROUP

    rows = -(-(1 + bs) // SUBLANES) * SUBLANES
    mods = jnp.zeros((rows, d), F32).at[0].set(c_ctx).at[1:1 + bs].set(c)
    ada = _ada(mods, ada_w, ada_b).reshape(depth, rows, 1, 3 * d)
    head_id = jnp.arange(GROUP_W) // HEAD
    row3 = lambda a: a.reshape(depth, 1, -1)
    p = {'g_pre': row3(g_pre), 'g_post': row3(g_post), 'w_in': w_in.astype(BF16), 'conv_a': conv_a,
         'conv_b': conv_b, 'conv_b_bias': row3(conv_b_bias), 'ln_b_g': row3(ln_b_g),
         'ln_b_b': row3(ln_b_b), 'mu': row3(mu), 'w0': w0, 'a0': a0,
         'w2': w2.reshape(depth, 2 * LORA, wc).astype(BF16),
         'a2': a2.reshape(depth, 2 * LORA, wc).astype(BF16),
         'k_k': row3(k_k), 'k_a': row3(k_a), 'r_k': row3(r_k), 'gn_g': row3(gn_g),
         'gn_b': row3(gn_b), 'w_out': w_out.astype(BF16),
         'ones_g': (head_id[:, None] == head_id[None, :]).astype(BF16)}
    s0 = jnp.tile(state_rwkv, (1, 1, 1, 1, 1, HEADS_PER_GROUP))
    s0 = s0.reshape(bs, depth, 2, n_groups, GROUP_W, GROUP_W)

    ctx_row = (0, False)
    lat_row = (1, True)
    xp = x_prompt.reshape(bp * seq_p, d)
    xs = x_sample.reshape(bs * seq_s, d)
    new_states = []
    for l in range(depth):
        xp, st = _trunk_layer(xp, ada, ctx_row, p, l, None, seq_p, False)
        new_states.append(_unpack_states(st))
        xs, _ = _trunk_layer(xs, ada, lat_row, p, l, s0, seq_s, True)
    return (xp.reshape(bp, seq_p, d), xs.reshape(bs, seq_s, d), jnp.stack(new_states, axis=1))
```
